```python
import jax, jax.numpy as jnp
from jax import lax
import numpy as np

D_MODEL = 1024
BATCH = 8
SEQ = 8192
DEPTH = 1

HEAD_DIM = 64
HEADS_PER_GROUP = 4
DILATED_GROUPS = ((128, 1), (512, 4), (2048, 16))
N_GROUPS = len(DILATED_GROUPS)
ATTN_HEADS = HEADS_PER_GROUP * N_GROUPS
ATTN_WIDTH = ATTN_HEADS * HEAD_DIM
ROPE_THETA = 500000.0
ROPE_DIM = HEAD_DIM // 4
CONV_WIDTH = D_MODEL
CONV_KERNEL = 31
D_FF = 4 * D_MODEL
N_BRANCHES = 2
IN_WIDTH = 2 * CONV_WIDTH + 3 * ATTN_WIDTH + N_BRANCHES * D_MODEL
NORM_EPS = 1e-6
MASK_VALUE = -1e30

kernel_name = "hybrid_conformer_dilated_attn_block"


def rmsnorm(x, w):
    xf = x.astype(jnp.float32)
    xf = xf * lax.rsqrt(jnp.mean(xf * xf, axis=-1, keepdims=True) + NORM_EPS)
    return (xf * w.astype(jnp.float32)).astype(x.dtype)


def partial_rope(x, positions):
    half = ROPE_DIM // 2
    inv_freq = ROPE_THETA ** (-jnp.arange(0, ROPE_DIM, 2, dtype=jnp.float32) / ROPE_DIM)
    ang = positions.astype(jnp.float32)[..., None] * inv_freq
    cos = jnp.cos(ang)[:, :, None, :]
    sin = jnp.sin(ang)[:, :, None, :]
    xf = x.astype(jnp.float32)
    x1, x2, rest = xf[..., :half], xf[..., half:ROPE_DIM], xf[..., ROPE_DIM:]
    out = jnp.concatenate([x1 * cos - x2 * sin, x2 * cos + x1 * sin, rest], axis=-1)
    return out.astype(x.dtype)


def dilated_window_attention(q, k, v, window, dilation):
    B, T, H, Dh = q.shape
    span = window // dilation
    Tp = T // dilation
    nb = -(-Tp // span)
    pad = nb * span - Tp

    def to_phase(a):
        a = a.reshape(B, Tp, dilation, H, Dh).transpose(0, 2, 1, 3, 4).reshape(B * dilation, Tp, H, Dh)
        a = jnp.pad(a, ((0, 0), (0, pad), (0, 0), (0, 0)))
        return a.reshape(B * dilation, nb, span, H, Dh)

    def with_prev(a):
        prev = jnp.pad(a[:, :-1], ((0, 0), (1, 0), (0, 0), (0, 0), (0, 0)))
        return jnp.concatenate([prev, a], axis=2)

    qb = to_phase(q)
    kw = with_prev(to_phase(k))
    vw = with_prev(to_phase(v))

    s = jnp.einsum('bnqhd,bnkhd->bnhqk', qb, kw).astype(jnp.float32) * (Dh ** -0.5)
    qi = jnp.arange(span)[:, None]
    ki = jnp.arange(2 * span)[None, :]
    dist = qi + span - ki
    blk = jnp.arange(nb)[:, None, None]
    valid = (dist >= 0) & (dist <= span) & ((blk > 0) | (ki >= span))
    s = jnp.where(valid[None, :, None], s, MASK_VALUE)
    m = jnp.max(s, axis=-1, keepdims=True)
    e = jnp.exp(s - m)
    l = jnp.sum(e, axis=-1, keepdims=True)
    o = jnp.einsum('bnhqk,bnkhd->bnqhd', (e / l).astype(v.dtype), vw)
    lse = (m + jnp.log(l))[..., 0]

    o = o.reshape(B * dilation, nb * span, H, Dh)[:, :Tp]
    o = o.reshape(B, dilation, Tp, H, Dh).transpose(0, 2, 1, 3, 4).reshape(B, T, H, Dh)
    lse = lse.transpose(0, 1, 3, 2).reshape(B * dilation, nb * span, H)[:, :Tp]
    lse = lse.reshape(B, dilation, Tp, H).transpose(0, 2, 1, 3).reshape(B, T, H)
    return o, lse


def conformer_conv(conv_in, w_dw, b_dw, w_conv_norm, w_conv_out):
    u = conv_in[..., :CONV_WIDTH] * jax.nn.sigmoid(conv_in[..., CONV_WIDTH:])
    u = jnp.pad(u, ((0, 0), (CONV_KERNEL - 1, 0), (0, 0)))
    y = lax.conv_general_dilated(
        u, w_dw[:, None, :].astype(u.dtype), window_strides=(1,), padding='VALID',
        dimension_numbers=('NWC', 'WIO', 'NWC'), feature_group_count=CONV_WIDTH) + b_dw
    y = rmsnorm(y, w_conv_norm)
    y = jax.nn.silu(y)
    return y @ w_conv_out


def dilated_attention_branch(q, k, v, positions, q_norm_w, k_norm_w, w_attn_out):
    B, T = q.shape[:2]
    q = q.reshape(B, T, ATTN_HEADS, HEAD_DIM)
    k = k.reshape(B, T, ATTN_HEADS, HEAD_DIM)
    v = v.reshape(B, T, ATTN_HEADS, HEAD_DIM)
    q = partial_rope(rmsnorm(q, q_norm_w), positions)
    k = partial_rope(rmsnorm(k, k_norm_w), positions)
    outs, lses = [], []
    for g, (window, dilation) in enumerate(DILATED_GROUPS):
        hs = slice(g * HEADS_PER_GROUP, (g + 1) * HEADS_PER_GROUP)
        o, lse = dilated_window_attention(q[:, :, hs], k[:, :, hs], v[:, :, hs], window, dilation)
        outs.append(o)
        lses.append(lse)
    alpha = jax.nn.softmax(jnp.stack(lses, axis=0), axis=0)
    o = jnp.concatenate([outs[g] * alpha[g][..., None].astype(outs[g].dtype) for g in range(N_GROUPS)], axis=2)
    return o.reshape(B, T, ATTN_WIDTH) @ w_attn_out


def setup_inputs(seed: int = 0) -> dict:
    key = jax.random.key(seed)
    ks = jax.random.split(key, 20)
    f32 = jnp.float32

    def normal(k, shape, fan_in):
        return jax.random.normal(k, shape, f32) * (fan_in ** -0.5)

    def gain(k, shape):
        return 1.0 + 0.01 * jax.random.normal(k, shape, f32)

    x = jax.random.normal(ks[0], (BATCH, SEQ, D_MODEL), f32)
    offset = jax.random.randint(ks[1], (BATCH, 1), 0, 1024, dtype=jnp.int32)
    positions = offset + jnp.arange(SEQ, dtype=jnp.int32)[None, :]
    return {
        "x": x,
        "positions": positions,
        "w_norm_mix": gain(ks[2], (DEPTH, D_MODEL)),
        "w_in": normal(ks[3], (DEPTH, D_MODEL, IN_WIDTH), D_MODEL),
        "b_gate": 0.01 * jax.random.normal(ks[4], (DEPTH, N_BRANCHES * D_MODEL), f32),
        "w_dw": normal(ks[5], (DEPTH, CONV_KERNEL, CONV_WIDTH), CONV_KERNEL),
        "b_dw": 0.01 * jax.random.normal(ks[6], (DEPTH, CONV_WIDTH), f32),
        "w_conv_norm": gain(ks[7], (DEPTH, CONV_WIDTH)),
        "w_conv_out": normal(ks[8], (DEPTH, CONV_WIDTH, D_MODEL), CONV_WIDTH),
        "q_norm_w": gain(ks[9], (DEPTH, HEAD_DIM)),
        "k_norm_w": gain(ks[10], (DEPTH, HEAD_DIM)),
        "w_attn_out": normal(ks[11], (DEPTH, ATTN_WIDTH, D_MODEL), ATTN_WIDTH),
        "w_o": normal(ks[12], (DEPTH, D_MODEL, D_MODEL), D_MODEL),
        "w_norm_mlp": gain(ks[13], (DEPTH, D_MODEL)),
        "w_mlp_in": normal(ks[14], (DEPTH, D_MODEL, D_FF), D_MODEL),
        "w_mlp_out": normal(ks[15], (DEPTH, D_FF, D_MODEL), D_FF),
    }


def reference(x, positions, w_norm_mix, w_in, b_gate, w_dw, b_dw, w_conv_norm, w_conv_out,
              q_norm_w, k_norm_w, w_attn_out, w_o, w_norm_mlp, w_mlp_in, w_mlp_out):
    c0 = 2 * CONV_WIDTH
    c1 = c0 + ATTN_WIDTH
    c2 = c1 + ATTN_WIDTH
    c3 = c2 + ATTN_WIDTH
    for layer in range(DEPTH):
        n = rmsnorm(x, w_norm_mix[layer])
        proj = n @ w_in[layer]
        branch_a = conformer_conv(proj[..., :c0], w_dw[layer], b_dw[layer],
                                  w_conv_norm[layer], w_conv_out[layer])
        branch_b = dilated_attention_branch(proj[..., c0:c1], proj[..., c1:c2], proj[..., c2:c3],
                                            positions, q_norm_w[layer], k_norm_w[layer],
                                            w_attn_out[layer])
        gates = jax.nn.sigmoid(proj[..., c3:] + b_gate[layer])
        merged = gates[..., :D_MODEL] * branch_a + gates[..., D_MODEL:] * branch_b
        x = x + merged @ w_o[layer]
        h = rmsnorm(x, w_norm_mlp[layer]) @ w_mlp_in[layer]
        x = x + jnp.square(jax.nn.relu(h)) @ w_mlp_out[layer]
    return x
```

```python
import functools

import jax
import jax.numpy as jnp
from jax import lax
from jax.experimental import pallas as pl
from jax.experimental.pallas import tpu as pltpu

D_MODEL = 1024
HEAD_DIM = 64
HEADS_PER_GROUP = 4
GROUP_WIDTH = HEADS_PER_GROUP * HEAD_DIM
DILATIONS = (1, 4, 16)
SPAN = 128
N_GROUPS = len(DILATIONS)
ATTN_WIDTH = N_GROUPS * GROUP_WIDTH
ROPE_THETA = 500000.0
ROPE_DIM = HEAD_DIM // 4
CONV_WIDTH = D_MODEL
CONV_KERNEL = 31
D_FF = 4 * D_MODEL
NORM_EPS = 1e-6
MASK_VALUE = -1e30

LANES = 128
CONV_HALO = 32
ATTN_TILE = SPAN * DILATIONS[-1]
TOKEN_TILE = 512
CONV_ROWS = 64
FF_CHUNK = 512
VMEM_LIMIT = 56 * 1024 * 1024

_F32 = jnp.float32
_BF16 = jnp.bfloat16


def _dot(a, b):
    return jnp.dot(a, b, preferred_element_type=_F32)


def _sigmoid(z):
    return 1.0 / (1.0 + jnp.exp(-z))


def _rms_scale(xf):
    return lax.rsqrt(jnp.mean(xf * xf, axis=-1, keepdims=True) + NORM_EPS)


def _inproj_kernel(x_ref, pos_ref, wn_ref, w_ref, bg_ref, qkw_ref, freq_ref, seg_ref,
                   u_ref, q1_ref, q2_ref, q3_ref, k1_ref, k2_ref, k3_ref, v1_ref, v2_ref, v3_ref, g_ref):
    x = x_ref[...]
    n = (x * _rms_scale(x) * wn_ref[...]).astype(_BF16)

    c0 = 2 * CONV_WIDTH
    a = _dot(n, w_ref[:, 0:CONV_WIDTH])
    b = _dot(n, w_ref[:, CONV_WIDTH:c0])
    u_ref[...] = (a * _sigmoid(b)).astype(_BF16)

    ang = pos_ref[...] * freq_ref[0:1, :]
    cos = jnp.cos(ang)
    sin = jnp.sin(ang) * freq_ref[1:2, :]
    first_half = freq_ref[2:3, :] > 0.5
    seg = seg_ref[...]

    def qk_head_norm_rope(col, gain_row, out_ref, scale):
        z = _dot(n, w_ref[:, col:col + GROUP_WIDTH])
        ss = _dot((z * z).astype(_BF16), seg)
        z = z * lax.rsqrt(ss * (1.0 / HEAD_DIM) + NORM_EPS) * qkw_ref[gain_row:gain_row + 1, :]
        for t in range(GROUP_WIDTH // LANES):
            zt = z[:, t * LANES:(t + 1) * LANES]
            partner = jnp.where(first_half, pltpu.roll(zt, LANES - ROPE_DIM // 2, 1),
                                pltpu.roll(zt, ROPE_DIM // 2, 1))
            rt = zt * cos + partner * sin
            out_ref[:, t * LANES:(t + 1) * LANES] = (rt * scale).astype(_BF16)

    q_refs = (q1_ref, q2_ref, q3_ref)
    k_refs = (k1_ref, k2_ref, k3_ref)
    v_refs = (v1_ref, v2_ref, v3_ref)
    for g in range(N_GROUPS):
        qk_head_norm_rope(c0 + g * GROUP_WIDTH, 0, q_refs[g], HEAD_DIM ** -0.5)
        qk_head_norm_rope(c0 + ATTN_WIDTH + g * GROUP_WIDTH, 1, k_refs[g], 1.0)
        col = c0 + 2 * ATTN_WIDTH + g * GROUP_WIDTH
        v_refs[g][...] = _dot(n, w_ref[:, col:col + GROUP_WIDTH]).astype(_BF16)

    c3 = c0 + 3 * ATTN_WIDTH
    for t in range(2):
        col = c3 + t * D_MODEL
        z = _dot(n, w_ref[:, col:col + D_MODEL]) + bg_ref[:, t * D_MODEL:(t + 1) * D_MODEL]
        g_ref[:, t * D_MODEL:(t + 1) * D_MODEL] = _sigmoid(z).astype(_BF16)


def _attn_kernel(q1_ref, k1_ref, k1p_ref, v1_ref, v1p_ref,
                 q2_ref, k2_ref, k2p_ref, v2_ref, v2p_ref,
                 q3_ref, k3_ref, k3p_ref, v3_ref, v3p_ref,
                 o1_ref, o2_ref, o3_ref, l1_ref, l2_ref, l3_ref,
                 kx1, vx1, kx2, vx2, kx3, vx3):
    tile = pl.program_id(1)

    for kx, vx, kp, kc, vp, vc in ((kx1, vx1, k1p_ref, k1_ref, v1p_ref, v1_ref),
                                   (kx2, vx2, k2p_ref, k2_ref, v2p_ref, v2_ref),
                                   (kx3, vx3, k3p_ref, k3_ref, v3p_ref, v3_ref)):
        kx[0:SPAN, :] = kp[0]
        kx[SPAN:, :] = kc[0]
        vx[0:SPAN, :] = vp[0]
        vx[SPAN:, :] = vc[0]

    qi = lax.broadcasted_iota(jnp.int32, (SPAN, 2 * SPAN), 0)
    ki = lax.broadcasted_iota(jnp.int32, (SPAN, 2 * SPAN), 1)
    dist = qi + SPAN - ki
    band = jnp.where((dist >= 0) & (dist <= SPAN), 0.0, MASK_VALUE).astype(_F32)
    band_first = jnp.where((ki >= SPAN) | (tile > 0), band, MASK_VALUE)

    lane = lax.broadcasted_iota(jnp.int32, (SPAN, LANES), 1)
    low_head = lane < HEAD_DIM

    def unit(q_ref, kx, vx, o_ref, l_ref, row0, col0, bias):
        for pair in range(GROUP_WIDTH // LANES):
            c = col0 + pair * LANES
            q = q_ref[0, pl.ds(row0, SPAN), c:c + LANES]
            kk = kx[pl.ds(row0, 2 * SPAN), c:c + LANES]
            vv = vx[pl.ds(row0, 2 * SPAN), c:c + LANES]
            outs, lses = [], []
            for keep in (low_head, ~low_head):
                qh = jnp.where(keep, q, jnp.zeros_like(q))
                s = lax.dot_general(qh, kk, (((1,), (1,)), ((), ())), preferred_element_type=_F32) + bias
                m = jnp.max(s, axis=-1, keepdims=True)
                e = jnp.exp(s - m)
                l = jnp.sum(e, axis=-1, keepdims=True)
                outs.append(_dot(e.astype(_BF16), vv) * (1.0 / l))
                lses.append(jnp.broadcast_to(m + jnp.log(l), (SPAN, LANES)))
            o_ref[0, pl.ds(row0, SPAN), c:c + LANES] = jnp.where(low_head, outs[0], outs[1]).astype(_BF16)
            l_ref[0, pl.ds(row0, SPAN), c:c + LANES] = jnp.where(low_head, lses[0], lses[1])

    unit(q1_ref, kx1, vx1, o1_ref, l1_ref, 0, 0, band_first)

    def g1_body(j, carry):
        unit(q1_ref, kx1, vx1, o1_ref, l1_ref, pl.multiple_of(j * SPAN, SPAN), 0, band)
        return carry

    lax.fori_loop(1, ATTN_TILE // SPAN, g1_body, 0)

    n2 = ATTN_TILE // DILATIONS[1] // SPAN
    for r in range(DILATIONS[1]):
        unit(q2_ref, kx2, vx2, o2_ref, l2_ref, 0, r * GROUP_WIDTH, band_first)

        def g2_body(j, carry, r=r):
            unit(q2_ref, kx2, vx2, o2_ref, l2_ref, pl.multiple_of(j * SPAN, SPAN), r * GROUP_WIDTH, band)
            return carry

        lax.fori_loop(1, n2, g2_body, 0)

    for r in range(DILATIONS[2]):
        unit(q3_ref, kx3, vx3, o3_ref, l3_ref, 0, r * GROUP_WIDTH, band_first)


def _mix_kernel(x_ref, u_ref, uh_ref, o1_ref, o2_ref, o3_ref, l1_ref, l2_ref, l3_ref, g_ref,
                wdw_ref, bdw_ref, wcn_ref, wco_ref, wao_ref, wo_ref,
                x1_ref, ext_ref, y_ref, *, tiles_per_seq):
    first = (pl.program_id(0) % tiles_per_seq) == 0
    tm = x_ref.shape[0]

    halo = uh_ref[...].astype(_F32)
    ext_ref[0:CONV_HALO, :] = jnp.where(first, jnp.zeros_like(halo), halo)
    ext_ref[CONV_HALO:, :] = u_ref[...].astype(_F32)

    shift = CONV_HALO - (CONV_KERNEL - 1)
    for c in range(CONV_WIDTH // LANES):
        cols = slice(c * LANES, (c + 1) * LANES)
        taps = [jnp.broadcast_to(wdw_ref[k:k + 1, cols], (CONV_ROWS, LANES)) for k in range(CONV_KERNEL)]
        bias = jnp.broadcast_to(bdw_ref[:, cols], (CONV_ROWS, LANES))

        for base in range(0, tm, CONV_ROWS):
            acc = bias
            for k in range(CONV_KERNEL):
                acc = acc + ext_ref[base + shift + k:base + shift + k + CONV_ROWS, cols] * taps[k]
            y_ref[base:base + CONV_ROWS, cols] = acc

    y = y_ref[...]
    y = y * _rms_scale(y) * wcn_ref[...]
    y = y * _sigmoid(y)
    branch_a = _dot(y.astype(_BF16), wco_ref[...])

    l1, l2, l3 = l1_ref[...], l2_ref[...], l3_ref[...]
    mx = jnp.maximum(jnp.maximum(l1, l2), l3)
    e1, e2, e3 = jnp.exp(l1 - mx), jnp.exp(l2 - mx), jnp.exp(l3 - mx)
    inv = 1.0 / (e1 + e2 + e3)
    branch_b = jnp.zeros((tm, D_MODEL), _F32)
    for g, (o_ref, e) in enumerate(((o1_ref, e1), (o2_ref, e2), (o3_ref, e3))):
        ow = (o_ref[...].astype(_F32) * (e * inv)).astype(_BF16)
        branch_b = branch_b + _dot(ow, wao_ref[g * GROUP_WIDTH:(g + 1) * GROUP_WIDTH, :])

    merged = (g_ref[:, 0:D_MODEL].astype(_F32) * branch_a
              + g_ref[:, D_MODEL:2 * D_MODEL].astype(_F32) * branch_b)
    x1_ref[...] = x_ref[...] + _dot(merged.astype(_BF16), wo_ref[...])


def _mlp_kernel(x_ref, wn_ref, w1_ref, w2_ref, out_ref):
    x = x_ref[...]
    n = (x * _rms_scale(x) * wn_ref[...]).astype(_BF16)
    acc = x
    for c in range(D_FF // FF_CHUNK):
        h = _dot(n, w1_ref[:, c * FF_CHUNK:(c + 1) * FF_CHUNK])
        h = jnp.maximum(h, 0.0)
        acc = acc + _dot((h * h).astype(_BF16), w2_ref[c * FF_CHUNK:(c + 1) * FF_CHUNK, :])
    out_ref[...] = acc


def _const_spec(shape):
    return pl.BlockSpec(shape, lambda *_: (0,) * len(shape))


def _rope_rows():
    d = jnp.arange(LANES) % HEAD_DIM
    half = ROPE_DIM // 2
    inv_freq = ROPE_THETA ** (-jnp.arange(0, ROPE_DIM, 2, dtype=_F32) / ROPE_DIM)
    freq = jnp.where(d < ROPE_DIM, inv_freq[d % half], 0.0)
    sign = jnp.where(d < half, -1.0, 1.0)
    first = (d < half).astype(_F32)
    rows = jnp.stack([freq, sign, first] + [jnp.zeros((LANES,), _F32)] * 5)
    return rows.astype(_F32)


def _params(n_axes):
    return pltpu.CompilerParams(dimension_semantics=("arbitrary",) * n_axes, vmem_limit_bytes=VMEM_LIMIT)


def kernel(x, positions, w_norm_mix, w_in, b_gate, w_dw, b_dw, w_conv_norm, w_conv_out, q_norm_w, k_norm_w,
           w_attn_out, w_o, w_norm_mlp, w_mlp_in, w_mlp_out):
    batch, seq, d_model = x.shape
    depth = w_in.shape[0]
    assert d_model == D_MODEL and seq % ATTN_TILE == 0 and ATTN_TILE % TOKEN_TILE == 0
    n_tok = batch * seq
    tm = TOKEN_TILE
    n_tiles = n_tok // tm
    in_width = w_in.shape[-1]

    xf = x.reshape(n_tok, d_model)
    pos = jnp.broadcast_to(positions.reshape(n_tok, 1).astype(_F32), (n_tok, LANES))
    rope_rows = _rope_rows()
    head = jnp.arange(GROUP_WIDTH) // HEAD_DIM
    seg = (head[:, None] == head[None, :]).astype(_BF16)

    def row_spec(width):
        return pl.BlockSpec((tm, width), lambda i: (i, 0))

    for layer in range(depth):
        qk_gain = jnp.stack([jnp.tile(q_norm_w[layer], GROUP_WIDTH // HEAD_DIM),
                             jnp.tile(k_norm_w[layer], GROUP_WIDTH // HEAD_DIM)]).astype(_F32)

        grp = jax.ShapeDtypeStruct((n_tok, GROUP_WIDTH), _BF16)
        outs = pl.pallas_call(
            _inproj_kernel,
            grid=(n_tiles,),
            in_specs=[row_spec(d_model), row_spec(LANES), _const_spec((1, d_model)),
                      _const_spec((d_model, in_width)), _const_spec((1, 2 * d_model)),
                      _const_spec((2, GROUP_WIDTH)), _const_spec((8, LANES)),
                      _const_spec((GROUP_WIDTH, GROUP_WIDTH))],
            out_specs=[row_spec(CONV_WIDTH)] + [row_spec(GROUP_WIDTH)] * 9 + [row_spec(2 * d_model)],
            out_shape=[jax.ShapeDtypeStruct((n_tok, CONV_WIDTH), _BF16)] + [grp] * 9
                      + [jax.ShapeDtypeStruct((n_tok, 2 * d_model), _BF16)],
            compiler_params=_params(1),
            name="inproj",
        )(xf, pos, w_norm_mix[layer].reshape(1, d_model), w_in[layer].astype(_BF16),
          b_gate[layer].reshape(1, 2 * d_model), qk_gain, rope_rows, seg)
        u, qkv, gates = outs[0], outs[1:10], outs[10]

        n_attn_tiles = seq // ATTN_TILE
        in_specs, operands, o_specs, o_shapes, scratch = [], [], [], [], []
        for g, dil in enumerate(DILATIONS):
            rows, width = ATTN_TILE // dil, dil * GROUP_WIDTH
            view = (batch, seq // dil, width)
            q, k, v = (qkv[g].reshape(view), qkv[3 + g].reshape(view), qkv[6 + g].reshape(view))
            cur = pl.BlockSpec((1, rows, width), lambda b, i: (b, i, 0))
            prev = pl.BlockSpec((1, SPAN, width),
                                lambda b, i, rows=rows: (b, jnp.maximum(i * (rows // SPAN) - 1, 0), 0))
            in_specs += [cur, cur, prev, cur, prev]
            operands += [q, k, k, v, v]
            o_specs.append(cur)
            o_shapes.append(jax.ShapeDtypeStruct(view, _BF16))
            scratch += [pltpu.VMEM((SPAN + rows, width), _BF16)] * 2
        l_shapes = [jax.ShapeDtypeStruct(s.shape, _F32) for s in o_shapes]
        attn = pl.pallas_call(
            _attn_kernel,
            grid=(batch, n_attn_tiles),
            in_specs=in_specs,
            out_specs=o_specs + o_specs,
            out_shape=o_shapes + l_shapes,
            scratch_shapes=scratch,
            compiler_params=_params(2),
            name="attention",
        )(*operands)
        o_tok = [a.reshape(n_tok, GROUP_WIDTH) for a in attn[:3]]
        l_tok = [a.reshape(n_tok, GROUP_WIDTH) for a in attn[3:]]

        halo_spec = pl.BlockSpec((CONV_HALO, CONV_WIDTH),
                                 lambda i: (jnp.maximum(i * (tm // CONV_HALO) - 1, 0), 0))
        x1 = pl.pallas_call(
            functools.partial(_mix_kernel, tiles_per_seq=seq // tm),
            grid=(n_tiles,),
            in_specs=[row_spec(d_model), row_spec(CONV_WIDTH), halo_spec]
                     + [row_spec(GROUP_WIDTH)] * 6 + [row_spec(2 * d_model),
                        _const_spec((CONV_KERNEL, CONV_WIDTH)), _const_spec((1, CONV_WIDTH)),
                        _const_spec((1, CONV_WIDTH)), _const_spec((CONV_WIDTH, d_model)),
                        _const_spec((ATTN_WIDTH, d_model)), _const_spec((d_model, d_model))],
            out_specs=row_spec(d_model),
            out_shape=jax.ShapeDtypeStruct((n_tok, d_model), _F32),
            scratch_shapes=[pltpu.VMEM((CONV_HALO + tm, CONV_WIDTH), _F32),
                            pltpu.VMEM((tm, CONV_WIDTH), _F32)],
            compiler_params=_params(1),
            name="mixer_out",
        )(xf, u, u, *o_tok, *l_tok, gates, w_dw[layer], b_dw[layer].reshape(1, CONV_WIDTH),
          w_conv_norm[layer].reshape(1, CONV_WIDTH), w_conv_out[layer].astype(_BF16),
          w_attn_out[layer].astype(_BF16), w_o[layer].astype(_BF16))

        xf = pl.pallas_call(
            _mlp_kernel,
            grid=(n_tiles,),
            in_specs=[row_spec(d_model), _const_spec((1, d_model)),
                      _const_spec((d_model, D_FF)), _const_spec((D_FF, d_model))],
            out_specs=row_spec(d_model),
            out_shape=jax.ShapeDtypeStruct((n_tok, d_model), _F32),
            compiler_params=_params(1),
            name="mlp",
        )(x1, w_norm_mlp[layer].reshape(1, d_model), w_mlp_in[layer].astype(_BF16),
          w_mlp_out[layer].astype(_BF16))

    return xf.reshape(batch, seq, d_model)
```

```python
import functools

import jax
import jax.numpy as jnp
from jax import lax
from jax.experimental import pallas as pl
from jax.experimental.pallas import tpu as pltpu

D_MODEL = 1024
HEAD_DIM = 64
HEADS_PER_GROUP = 4
GROUP_WIDTH = HEADS_PER_GROUP * HEAD_DIM
DILATIONS = (1, 4, 16)
SPAN = 128
N_GROUPS = len(DILATIONS)
ATTN_WIDTH = N_GROUPS * GROUP_WIDTH
ROPE_THETA = 500000.0
ROPE_DIM = HEAD_DIM // 4
CONV_WIDTH = D_MODEL
CONV_KERNEL = 31
D_FF = 4 * D_MODEL
NORM_EPS = 1e-6
MASK_VALUE = -1e30

LANES = 128
CONV_HALO = 32
ATTN_TILE = SPAN * DILATIONS[-1]
TOKEN_TILE = 512
CONV_ROWS = 64
FF_CHUNK = 512
COMBINE_ROWS = 256
VMEM_LIMIT = 56 * 1024 * 1024

_F32 = jnp.float32
_BF16 = jnp.bfloat16


def _dot(a, b):
    return jnp.dot(a, b, preferred_element_type=_F32)


def _sigmoid(z):
    return 1.0 / (1.0 + jnp.exp(-z))


def _rms_scale(xf):
    return lax.rsqrt(jnp.mean(xf * xf, axis=-1, keepdims=True) + NORM_EPS)


def _inproj_kernel(x_ref, pos_ref, wn_ref, w_ref, bg_ref, qkw_ref, freq_ref, seg_ref,
                   u_ref, q1_ref, q2_ref, q3_ref, k1_ref, k2_ref, k3_ref, v1_ref, v2_ref, v3_ref, g_ref,
                   z_ref):
    tm = x_ref.shape[0]
    x = x_ref[...]
    n = (x * _rms_scale(x) * wn_ref[...]).astype(_BF16)

    c0 = 2 * CONV_WIDTH
    a = _dot(n, w_ref[:, 0:CONV_WIDTH])
    b = _dot(n, w_ref[:, CONV_WIDTH:c0])
    u_ref[...] = (a * _sigmoid(b)).astype(_BF16)

    ang = pos_ref[...] * freq_ref[0:1, :]
    cos = jnp.cos(ang)
    sin = jnp.sin(ang) * freq_ref[1:2, :]
    first_half = freq_ref[2:3, :] > 0.5
    seg = seg_ref[...]

    def store_by_residue(out_ref, dil):
        for t in range(GROUP_WIDTH // LANES):
            for r in range(dil):
                rows = z_ref[t] if dil == 1 else z_ref[t, pl.ds(r, tm // dil, stride=dil), :]
                c = r * GROUP_WIDTH + t * LANES
                out_ref[:, c:c + LANES] = rows.astype(_BF16)

    def qk_head_norm_rope(col, gain_row, scale):
        z = _dot(n, w_ref[:, col:col + GROUP_WIDTH])
        ss = _dot((z * z).astype(_BF16), seg)
        z = z * lax.rsqrt(ss * (1.0 / HEAD_DIM) + NORM_EPS) * qkw_ref[gain_row:gain_row + 1, :]
        for t in range(GROUP_WIDTH // LANES):
            zt = z[:, t * LANES:(t + 1) * LANES]
            partner = jnp.where(first_half, pltpu.roll(zt, LANES - ROPE_DIM // 2, 1),
                                pltpu.roll(zt, ROPE_DIM // 2, 1))
            z_ref[t] = (zt * cos + partner * sin) * scale

    q_refs = (q1_ref, q2_ref, q3_ref)
    k_refs = (k1_ref, k2_ref, k3_ref)
    v_refs = (v1_ref, v2_ref, v3_ref)
    for g, dil in enumerate(DILATIONS):
        qk_head_norm_rope(c0 + g * GROUP_WIDTH, 0, HEAD_DIM ** -0.5)
        store_by_residue(q_refs[g], dil)
        qk_head_norm_rope(c0 + ATTN_WIDTH + g * GROUP_WIDTH, 1, 1.0)
        store_by_residue(k_refs[g], dil)
        col = c0 + 2 * ATTN_WIDTH + g * GROUP_WIDTH
        z = _dot(n, w_ref[:, col:col + GROUP_WIDTH])
        for t in range(GROUP_WIDTH // LANES):
            z_ref[t] = z[:, t * LANES:(t + 1) * LANES]
        store_by_residue(v_refs[g], dil)

    c3 = c0 + 3 * ATTN_WIDTH
    for t in range(2):
        col = c3 + t * D_MODEL
        z = _dot(n, w_ref[:, col:col + D_MODEL]) + bg_ref[:, t * D_MODEL:(t + 1) * D_MODEL]
        g_ref[:, t * D_MODEL:(t + 1) * D_MODEL] = _sigmoid(z).astype(_BF16)


def _attn_kernel(q1_ref, k1_ref, k1p_ref, v1_ref, v1p_ref,
                 q2_ref, k2_ref, k2p_ref, v2_ref, v2p_ref,
                 q3_ref, k3_ref, k3p_ref, v3_ref, v3p_ref,
                 ow_ref,
                 kx1, vx1, kx2, vx2, kx3, vx3, o_scr, l_scr):
    tile = pl.program_id(1)

    for kx, vx, kp, kc, vp, vc in ((kx1, vx1, k1p_ref, k1_ref, v1p_ref, v1_ref),
                                   (kx2, vx2, k2p_ref, k2_ref, v2p_ref, v2_ref),
                                   (kx3, vx3, k3p_ref, k3_ref, v3p_ref, v3_ref)):
        kx[0:SPAN, :] = kp[...]
        kx[SPAN:, :] = kc[...]
        vx[0:SPAN, :] = vp[...]
        vx[SPAN:, :] = vc[...]

    qi = lax.broadcasted_iota(jnp.int32, (SPAN, 2 * SPAN), 0)
    ki = lax.broadcasted_iota(jnp.int32, (SPAN, 2 * SPAN), 1)
    dist = qi + SPAN - ki
    band = jnp.where((dist >= 0) & (dist <= SPAN), 0.0, MASK_VALUE).astype(_F32)
    band_first = jnp.where((ki >= SPAN) | (tile > 0), band, MASK_VALUE)

    lane = lax.broadcasted_iota(jnp.int32, (SPAN, LANES), 1)
    low_head = lane < HEAD_DIM

    def unit(q_ref, kx, vx, row0, col0, bias, tok_rows, out_tile0):
        for pair in range(GROUP_WIDTH // LANES):
            c = col0 + pair * LANES
            q = q_ref[pl.ds(row0, SPAN), c:c + LANES]
            kk = kx[pl.ds(row0, 2 * SPAN), c:c + LANES]
            vv = vx[pl.ds(row0, 2 * SPAN), c:c + LANES]
            outs, lses = [], []
            for keep in (low_head, ~low_head):
                qh = jnp.where(keep, q, jnp.zeros_like(q))
                s = lax.dot_general(qh, kk, (((1,), (1,)), ((), ())), preferred_element_type=_F32) + bias
                m = jnp.max(s, axis=-1, keepdims=True)
                e = jnp.exp(s - m)
                l = jnp.sum(e, axis=-1, keepdims=True)
                outs.append(_dot(e.astype(_BF16), vv) * (1.0 / l))
                lses.append(jnp.broadcast_to(m + jnp.log(l), (SPAN, LANES)))
            ot = out_tile0 + pair
            o_scr[ot, tok_rows, :] = jnp.where(low_head, outs[0], outs[1])
            l_scr[ot, tok_rows, :] = jnp.where(low_head, lses[0], lses[1])

    unit(q1_ref, kx1, vx1, 0, 0, band_first, pl.ds(0, SPAN), 0)

    def g1_body(j, carry):
        row0 = pl.multiple_of(j * SPAN, SPAN)
        unit(q1_ref, kx1, vx1, row0, 0, band, pl.ds(row0, SPAN), 0)
        return carry

    lax.fori_loop(1, ATTN_TILE // SPAN, g1_body, 0)

    for g in (1, 2):
        dil = DILATIONS[g]
        q_ref, kx, vx = ((q2_ref, kx2, vx2), (q3_ref, kx3, vx3))[g - 1]
        for r in range(dil):
            for j in range(ATTN_TILE // dil // SPAN):
                unit(q_ref, kx, vx, j * SPAN, r * GROUP_WIDTH, band_first if j == 0 else band,
                     pl.ds(dil * SPAN * j + r, SPAN, stride=dil), g * (GROUP_WIDTH // LANES))

    def combine(i, carry):
        rows = pl.ds(pl.multiple_of(i * COMBINE_ROWS, COMBINE_ROWS), COMBINE_ROWS)
        tiles_per_group = GROUP_WIDTH // LANES
        for t in range(tiles_per_group):
            ls = [l_scr[g * tiles_per_group + t, rows, :] for g in range(N_GROUPS)]
            mx = jnp.maximum(jnp.maximum(ls[0], ls[1]), ls[2])
            es = [jnp.exp(l - mx) for l in ls]
            inv = 1.0 / (es[0] + es[1] + es[2])
            for g in range(N_GROUPS):
                ot = g * tiles_per_group + t
                ow_ref[rows, ot * LANES:(ot + 1) * LANES] = (o_scr[ot, rows, :] * (es[g] * inv)).astype(_BF16)
        return carry

    lax.fori_loop(0, ATTN_TILE // COMBINE_ROWS, combine, 0)


def _mix_kernel(x_ref, u_ref, uh_ref, ow_ref, g_ref,
                wdw_ref, bdw_ref, wcn_ref, wco_ref, wao_ref, wo_ref,
                x1_ref, ext_ref, y_ref, *, tiles_per_seq):
    first = (pl.program_id(0) % tiles_per_seq) == 0
    tm = x_ref.shape[0]

    halo = uh_ref[...].astype(_F32)
    ext_ref[0:CONV_HALO, :] = jnp.where(first, jnp.zeros_like(halo), halo)
    ext_ref[CONV_HALO:, :] = u_ref[...].astype(_F32)

    shift = CONV_HALO - (CONV_KERNEL - 1)
    for c in range(CONV_WIDTH // LANES):
        cols = slice(c * LANES, (c + 1) * LANES)
        taps = [jnp.broadcast_to(wdw_ref[k:k + 1, cols], (CONV_ROWS, LANES)) for k in range(CONV_KERNEL)]
        bias = jnp.broadcast_to(bdw_ref[:, cols], (CONV_ROWS, LANES))

        for base in range(0, tm, CONV_ROWS):
            acc = bias
            for k in range(CONV_KERNEL):
                acc = acc + ext_ref[base + shift + k:base + shift + k + CONV_ROWS, cols] * taps[k]
            y_ref[base:base + CONV_ROWS, cols] = acc

    y = y_ref[...]
    y = y * _rms_scale(y) * wcn_ref[...]
    y = y * _sigmoid(y)
    branch_a = _dot(y.astype(_BF16), wco_ref[...])

    branch_b = _dot(ow_ref[...], wao_ref[...])

    merged = (g_ref[:, 0:D_MODEL].astype(_F32) * branch_a
              + g_ref[:, D_MODEL:2 * D_MODEL].astype(_F32) * branch_b)
    x1_ref[...] = x_ref[...] + _dot(merged.astype(_BF16), wo_ref[...])


def _mlp_kernel(x_ref, wn_ref, w1_ref, w2_ref, out_ref):
    x = x_ref[...]
    n = (x * _rms_scale(x) * wn_ref[...]).astype(_BF16)
    acc = x
    for c in range(D_FF // FF_CHUNK):
        h = _dot(n, w1_ref[:, c * FF_CHUNK:(c + 1) * FF_CHUNK])
        h = jnp.maximum(h, 0.0)
        acc = acc + _dot((h * h).astype(_BF16), w2_ref[c * FF_CHUNK:(c + 1) * FF_CHUNK, :])
    out_ref[...] = acc


def _const_spec(shape):
    return pl.BlockSpec(shape, lambda *_: (0,) * len(shape))


def _rope_rows():
    d = jnp.arange(LANES) % HEAD_DIM
    half = ROPE_DIM // 2
    inv_freq = ROPE_THETA ** (-jnp.arange(0, ROPE_DIM, 2, dtype=_F32) / ROPE_DIM)
    freq = jnp.where(d < ROPE_DIM, inv_freq[d % half], 0.0)
    sign = jnp.where(d < half, -1.0, 1.0)
    first = (d < half).astype(_F32)
    rows = jnp.stack([freq, sign, first] + [jnp.zeros((LANES,), _F32)] * 5)
    return rows.astype(_F32)


def _params(n_axes):
    return pltpu.CompilerParams(dimension_semantics=("arbitrary",) * n_axes, vmem_limit_bytes=VMEM_LIMIT)


def kernel(x, positions, w_norm_mix, w_in, b_gate, w_dw, b_dw, w_conv_norm, w_conv_out, q_norm_w, k_norm_w,
           w_attn_out, w_o, w_norm_mlp, w_mlp_in, w_mlp_out):
    batch, seq, d_model = x.shape
    depth = w_in.shape[0]
    assert d_model == D_MODEL and seq % ATTN_TILE == 0 and ATTN_TILE % TOKEN_TILE == 0
    n_tok = batch * seq
    tm = TOKEN_TILE
    n_tiles = n_tok // tm
    in_width = w_in.shape[-1]

    xf = x.reshape(n_tok, d_model)
    pos = jnp.broadcast_to(positions.reshape(n_tok, 1).astype(_F32), (n_tok, LANES))
    rope_rows = _rope_rows()
    head = jnp.arange(GROUP_WIDTH) // HEAD_DIM
    seg = (head[:, None] == head[None, :]).astype(_BF16)

    def row_spec(width):
        return pl.BlockSpec((tm, width), lambda i: (i, 0))

    for layer in range(depth):
        qk_gain = jnp.stack([jnp.tile(q_norm_w[layer], GROUP_WIDTH // HEAD_DIM),
                             jnp.tile(k_norm_w[layer], GROUP_WIDTH // HEAD_DIM)]).astype(_F32)

        def grp_spec(dil):
            return pl.BlockSpec((tm // dil, dil * GROUP_WIDTH), lambda i: (i, 0))

        grp_specs = [grp_spec(dil) for dil in DILATIONS] * 3
        grp_shapes = [jax.ShapeDtypeStruct((n_tok // dil, dil * GROUP_WIDTH), _BF16) for dil in DILATIONS] * 3
        outs = pl.pallas_call(
            _inproj_kernel,
            grid=(n_tiles,),
            in_specs=[row_spec(d_model), row_spec(LANES), _const_spec((1, d_model)),
                      _const_spec((d_model, in_width)), _const_spec((1, 2 * d_model)),
                      _const_spec((2, GROUP_WIDTH)), _const_spec((8, LANES)),
                      _const_spec((GROUP_WIDTH, GROUP_WIDTH))],
            out_specs=[row_spec(CONV_WIDTH)] + grp_specs + [row_spec(2 * d_model)],
            out_shape=[jax.ShapeDtypeStruct((n_tok, CONV_WIDTH), _BF16)] + grp_shapes
                      + [jax.ShapeDtypeStruct((n_tok, 2 * d_model), _BF16)],
            scratch_shapes=[pltpu.VMEM((GROUP_WIDTH // LANES, tm, LANES), _F32)],
            compiler_params=_params(1),
            name="inproj",
        )(xf, pos, w_norm_mix[layer].reshape(1, d_model), w_in[layer].astype(_BF16),
          b_gate[layer].reshape(1, 2 * d_model), qk_gain, rope_rows, seg)
        u, qkv, gates = outs[0], outs[1:10], outs[10]

        n_attn_tiles = seq // ATTN_TILE
        in_specs, operands, scratch = [], [], []
        for g, dil in enumerate(DILATIONS):
            rows, width = ATTN_TILE // dil, dil * GROUP_WIDTH
            spans_per_tile, spans_per_seq = rows // SPAN, seq // dil // SPAN
            cur = pl.BlockSpec((rows, width), lambda b, i: (b * n_attn_tiles + i, 0))
            prev = pl.BlockSpec(
                (SPAN, width),
                lambda b, i, n=spans_per_tile, s=spans_per_seq: (b * s + jnp.maximum(i * n - 1, 0), 0))
            in_specs += [cur, cur, prev, cur, prev]
            operands += [qkv[g], qkv[3 + g], qkv[3 + g], qkv[6 + g], qkv[6 + g]]
            scratch += [pltpu.VMEM((SPAN + rows, width), _BF16)] * 2
        scratch += [pltpu.VMEM((ATTN_WIDTH // LANES, ATTN_TILE, LANES), _F32)] * 2
        ow = pl.pallas_call(
            _attn_kernel,
            grid=(batch, n_attn_tiles),
            in_specs=in_specs,
            out_specs=pl.BlockSpec((ATTN_TILE, ATTN_WIDTH), lambda b, i: (b * n_attn_tiles + i, 0)),
            out_shape=jax.ShapeDtypeStruct((n_tok, ATTN_WIDTH), _BF16),
            scratch_shapes=scratch,
            compiler_params=_params(2),
            name="attention",
        )(*operands)

        halo_spec = pl.BlockSpec((CONV_HALO, CONV_WIDTH),
                                 lambda i: (jnp.maximum(i * (tm // CONV_HALO) - 1, 0), 0))
        x1 = pl.pallas_call(
            functools.partial(_mix_kernel, tiles_per_seq=seq // tm),
            grid=(n_tiles,),
            in_specs=[row_spec(d_model), row_spec(CONV_WIDTH), halo_spec, row_spec(ATTN_WIDTH),
                      row_spec(2 * d_model),
                      _const_spec((CONV_KERNEL, CONV_WIDTH)), _const_spec((1, CONV_WIDTH)),
                      _const_spec((1, CONV_WIDTH)), _const_spec((CONV_WIDTH, d_model)),
                      _const_spec((ATTN_WIDTH, d_model)), _const_spec((d_model, d_model))],
            out_specs=row_spec(d_model),
            out_shape=jax.ShapeDtypeStruct((n_tok, d_model), _F32),
            scratch_shapes=[pltpu.VMEM((CONV_HALO + tm, CONV_WIDTH), _F32),
                            pltpu.VMEM((tm, CONV_WIDTH), _F32)],
            compiler_params=_params(1),
            name="mixer_out",
        )(xf, u, u, ow, gates, w_dw[layer], b_dw[layer].reshape(1, CONV_WIDTH),
          w_conv_norm[layer].reshape(1, CONV_WIDTH), w_conv_out[layer].astype(_BF16),
          w_attn_out[layer].astype(_BF16), w_o[layer].astype(_BF16))

        xf = pl.pallas_call(
            _mlp_kernel,
            grid=(n_tiles,),
            in_specs=[row_spec(d_model), _const_spec((1, d_model)),
                      _const_spec((d_model, D_FF)), _const_spec((D_FF, d_model))],
            out_specs=row_spec(d_model),
            out_shape=jax.ShapeDtypeStruct((n_tok, d_model), _F32),
            compiler_params=_params(1),
            name="mlp",
        )(x1, w_norm_mlp[layer].reshape(1, d_model), w_mlp_in[layer].astype(_BF16),
          w_mlp_out[layer].astype(_BF16))

    return xf.reshape(batch, seq, d_model)
```

```python
import functools

import jax
import jax.numpy as jnp
from jax import lax
from jax.experimental import pallas as pl
from jax.experimental.pallas import tpu as pltpu

D_MODEL = 1024
HEAD_DIM = 64
HEADS_PER_GROUP = 4
GROUP_WIDTH = HEADS_PER_GROUP * HEAD_DIM
DILATIONS = (1, 4, 16)
SPAN = 128
N_GROUPS = len(DILATIONS)
ATTN_WIDTH = N_GROUPS * GROUP_WIDTH
ROPE_THETA = 500000.0
ROPE_DIM = HEAD_DIM // 4
CONV_WIDTH = D_MODEL
CONV_KERNEL = 31
D_FF = 4 * D_MODEL
NORM_EPS = 1e-6
MASK_VALUE = -1e30

LANES = 128
CONV_HALO = 32
ATTN_TILE = SPAN * DILATIONS[-1]
TOKEN_TILE = 512
CONV_ROWS = 32
CONV_STRIDE = 4
FF_CHUNK = 512
COMBINE_ROWS = 256
VMEM_LIMIT = 56 * 1024 * 1024

_F32 = jnp.float32
_BF16 = jnp.bfloat16


def _dot(a, b):
    return jnp.dot(a, b, preferred_element_type=_F32)


def _sigmoid(z):
    return 1.0 / (1.0 + jnp.exp(-z))


def _rms_scale(xf):
    return lax.rsqrt(jnp.mean(xf * xf, axis=-1, keepdims=True) + NORM_EPS)


def _inproj_kernel(x_ref, pos_ref, wn_ref, w_ref, bg_ref, qkw_ref, freq_ref, seg_ref,
                   u_ref, q1_ref, q2_ref, q3_ref, k1_ref, k2_ref, k3_ref, v1_ref, v2_ref, v3_ref, g_ref,
                   z_ref):
    tm = x_ref.shape[0]
    x = x_ref[...]
    n = (x * _rms_scale(x) * wn_ref[...]).astype(_BF16)

    c0 = 2 * CONV_WIDTH
    a = _dot(n, w_ref[:, 0:CONV_WIDTH])
    b = _dot(n, w_ref[:, CONV_WIDTH:c0])
    u_ref[...] = (a * _sigmoid(b)).astype(_BF16)

    ang = pos_ref[...] * freq_ref[0:1, :]
    cos = jnp.cos(ang)
    sin = jnp.sin(ang) * freq_ref[1:2, :]
    first_half = freq_ref[2:3, :] > 0.5
    seg = seg_ref[...]

    def store_by_residue(out_ref, dil):
        for t in range(GROUP_WIDTH // LANES):
            for r in range(dil):
                rows = z_ref[t] if dil == 1 else z_ref[t, pl.ds(r, tm // dil, stride=dil), :]
                c = r * GROUP_WIDTH + t * LANES
                out_ref[:, c:c + LANES] = rows.astype(_BF16)

    def qk_head_norm_rope(col, gain_row, scale):
        z = _dot(n, w_ref[:, col:col + GROUP_WIDTH])
        ss = _dot((z * z).astype(_BF16), seg)
        z = z * lax.rsqrt(ss * (1.0 / HEAD_DIM) + NORM_EPS) * qkw_ref[gain_row:gain_row + 1, :]
        for t in range(GROUP_WIDTH // LANES):
            zt = z[:, t * LANES:(t + 1) * LANES]
            partner = jnp.where(first_half, pltpu.roll(zt, LANES - ROPE_DIM // 2, 1),
                                pltpu.roll(zt, ROPE_DIM // 2, 1))
            z_ref[t] = (zt * cos + partner * sin) * scale

    q_refs = (q1_ref, q2_ref, q3_ref)
    k_refs = (k1_ref, k2_ref, k3_ref)
    v_refs = (v1_ref, v2_ref, v3_ref)
    for g, dil in enumerate(DILATIONS):
        qk_head_norm_rope(c0 + g * GROUP_WIDTH, 0, HEAD_DIM ** -0.5)
        store_by_residue(q_refs[g], dil)
        qk_head_norm_rope(c0 + ATTN_WIDTH + g * GROUP_WIDTH, 1, 1.0)
        store_by_residue(k_refs[g], dil)
        col = c0 + 2 * ATTN_WIDTH + g * GROUP_WIDTH
        z = _dot(n, w_ref[:, col:col + GROUP_WIDTH])
        for t in range(GROUP_WIDTH // LANES):
            z_ref[t] = z[:, t * LANES:(t + 1) * LANES]
        store_by_residue(v_refs[g], dil)

    c3 = c0 + 3 * ATTN_WIDTH
    for t in range(2):
        col = c3 + t * D_MODEL
        z = _dot(n, w_ref[:, col:col + D_MODEL]) + bg_ref[:, t * D_MODEL:(t + 1) * D_MODEL]
        g_ref[:, t * D_MODEL:(t + 1) * D_MODEL] = _sigmoid(z).astype(_BF16)


def _attn_kernel(q1_ref, k1_ref, k1p_ref, v1_ref, v1p_ref,
                 q2_ref, k2_ref, k2p_ref, v2_ref, v2p_ref,
                 q3_ref, k3_ref, k3p_ref, v3_ref, v3p_ref,
                 ow_ref,
                 kx1, vx1, kx2, vx2, kx3, vx3, o_scr, l_scr):
    tile = pl.program_id(1)

    for kx, vx, kp, kc, vp, vc in ((kx1, vx1, k1p_ref, k1_ref, v1p_ref, v1_ref),
                                   (kx2, vx2, k2p_ref, k2_ref, v2p_ref, v2_ref),
                                   (kx3, vx3, k3p_ref, k3_ref, v3p_ref, v3_ref)):
        kx[0:SPAN, :] = kp[...]
        kx[SPAN:, :] = kc[...]
        vx[0:SPAN, :] = vp[...]
        vx[SPAN:, :] = vc[...]

    qi = lax.broadcasted_iota(jnp.int32, (SPAN, 2 * SPAN), 0)
    ki = lax.broadcasted_iota(jnp.int32, (SPAN, 2 * SPAN), 1)
    dist = qi + SPAN - ki
    band = jnp.where((dist >= 0) & (dist <= SPAN), 0.0, MASK_VALUE).astype(_F32)
    band_first = jnp.where((ki >= SPAN) | (tile > 0), band, MASK_VALUE)

    lane = lax.broadcasted_iota(jnp.int32, (SPAN, LANES), 1)
    low_head = lane < HEAD_DIM

    def unit(q_ref, kx, vx, row0, col0, bias, tok_rows, out_tile0):
        for pair in range(GROUP_WIDTH // LANES):
            c = col0 + pair * LANES
            q = q_ref[pl.ds(row0, SPAN), c:c + LANES]
            kk = kx[pl.ds(row0, 2 * SPAN), c:c + LANES]
            vv = vx[pl.ds(row0, 2 * SPAN), c:c + LANES]
            outs, lses = [], []
            for keep in (low_head, ~low_head):
                qh = jnp.where(keep, q, jnp.zeros_like(q))
                s = lax.dot_general(qh, kk, (((1,), (1,)), ((), ())), preferred_element_type=_F32) + bias
                m = jnp.max(s, axis=-1, keepdims=True)
                e = jnp.exp(s - m)
                l = jnp.sum(e, axis=-1, keepdims=True)
                outs.append(_dot(e.astype(_BF16), vv) * (1.0 / l))
                lses.append(jnp.broadcast_to(m + jnp.log(l), (SPAN, LANES)))
            ot = out_tile0 + pair
            o_scr[ot, tok_rows, :] = jnp.where(low_head, outs[0], outs[1])
            l_scr[ot, tok_rows, :] = jnp.where(low_head, lses[0], lses[1])

    unit(q1_ref, kx1, vx1, 0, 0, band_first, pl.ds(0, SPAN), 0)

    def g1_body(j, carry):
        row0 = pl.multiple_of(j * SPAN, SPAN)
        unit(q1_ref, kx1, vx1, row0, 0, band, pl.ds(row0, SPAN), 0)
        return carry

    lax.fori_loop(1, ATTN_TILE // SPAN, g1_body, 0)

    for g in (1, 2):
        dil = DILATIONS[g]
        q_ref, kx, vx = ((q2_ref, kx2, vx2), (q3_ref, kx3, vx3))[g - 1]
        for r in range(dil):
            for j in range(ATTN_TILE // dil // SPAN):
                unit(q_ref, kx, vx, j * SPAN, r * GROUP_WIDTH, band_first if j == 0 else band,
                     pl.ds(dil * SPAN * j + r, SPAN, stride=dil), g * (GROUP_WIDTH // LANES))

    def combine(i, carry):
        rows = pl.ds(pl.multiple_of(i * COMBINE_ROWS, COMBINE_ROWS), COMBINE_ROWS)
        tiles_per_group = GROUP_WIDTH // LANES
        for t in range(tiles_per_group):
            ls = [l_scr[g * tiles_per_group + t, rows, :] for g in range(N_GROUPS)]
            mx = jnp.maximum(jnp.maximum(ls[0], ls[1]), ls[2])
            es = [jnp.exp(l - mx) for l in ls]
            inv = 1.0 / (es[0] + es[1] + es[2])
            for g in range(N_GROUPS):
                ot = g * tiles_per_group + t
                ow_ref[rows, ot * LANES:(ot + 1) * LANES] = (o_scr[ot, rows, :] * (es[g] * inv)).astype(_BF16)
        return carry

    lax.fori_loop(0, ATTN_TILE // COMBINE_ROWS, combine, 0)


def _mix_kernel(x_ref, u_ref, uh_ref, ow_ref, g_ref,
                wdw_ref, bdw_ref, wcn_ref, wco_ref, wao_ref, wo_ref,
                x1_ref, ext_ref, y_ref, *, tiles_per_seq):
    first = (pl.program_id(0) % tiles_per_seq) == 0
    tm = x_ref.shape[0]

    shift = CONV_HALO - (CONV_KERNEL - 1)
    block = CONV_ROWS * CONV_STRIDE
    for c in range(CONV_WIDTH // LANES):
        cols = slice(c * LANES, (c + 1) * LANES)
        halo = uh_ref[:, cols].astype(_F32)
        ext_ref[c, 0:CONV_HALO, :] = jnp.where(first, jnp.zeros_like(halo), halo)
        ext_ref[c, CONV_HALO:, :] = u_ref[:, cols].astype(_F32)

        taps = [jnp.broadcast_to(wdw_ref[k:k + 1, cols], (CONV_ROWS, LANES)) for k in range(CONV_KERNEL)]
        bias = jnp.broadcast_to(bdw_ref[:, cols], (CONV_ROWS, LANES))

        for base in range(0, tm, block):
            accs = [bias] * CONV_STRIDE
            for s in range(CONV_KERNEL + CONV_STRIDE - 1):
                rows = ext_ref[c, pl.ds(base + shift + s, CONV_ROWS, stride=CONV_STRIDE), :]
                for phase in range(CONV_STRIDE):
                    if 0 <= s - phase < CONV_KERNEL:
                        accs[phase] = accs[phase] + rows * taps[s - phase]
            for phase in range(CONV_STRIDE):
                y_ref[c, pl.ds(base + phase, CONV_ROWS, stride=CONV_STRIDE), :] = accs[phase]

    y = jnp.concatenate([y_ref[c] for c in range(CONV_WIDTH // LANES)], axis=1)
    y = y * _rms_scale(y) * wcn_ref[...]
    y = y * _sigmoid(y)
    branch_a = _dot(y.astype(_BF16), wco_ref[...])

    branch_b = _dot(ow_ref[...], wao_ref[...])

    merged = (g_ref[:, 0:D_MODEL].astype(_F32) * branch_a
              + g_ref[:, D_MODEL:2 * D_MODEL].astype(_F32) * branch_b)
    x1_ref[...] = x_ref[...] + _dot(merged.astype(_BF16), wo_ref[...])


def _mlp_kernel(x_ref, wn_ref, w1_ref, w2_ref, out_ref):
    x = x_ref[...]
    n = (x * _rms_scale(x) * wn_ref[...]).astype(_BF16)
    acc = x
    for c in range(D_FF // FF_CHUNK):
        h = _dot(n, w1_ref[:, c * FF_CHUNK:(c + 1) * FF_CHUNK])
        h = jnp.maximum(h, 0.0)
        acc = acc + _dot((h * h).astype(_BF16), w2_ref[c * FF_CHUNK:(c + 1) * FF_CHUNK, :])
    out_ref[...] = acc


def _const_spec(shape):
    return pl.BlockSpec(shape, lambda *_: (0,) * len(shape))


def _rope_rows():
    d = jnp.arange(LANES) % HEAD_DIM
    half = ROPE_DIM // 2
    inv_freq = ROPE_THETA ** (-jnp.arange(0, ROPE_DIM, 2, dtype=_F32) / ROPE_DIM)
    freq = jnp.where(d < ROPE_DIM, inv_freq[d % half], 0.0)
    sign = jnp.where(d < half, -1.0, 1.0)
    first = (d < half).astype(_F32)
    rows = jnp.stack([freq, sign, first] + [jnp.zeros((LANES,), _F32)] * 5)
    return rows.astype(_F32)


def _params(n_axes):
    return pltpu.CompilerParams(dimension_semantics=("arbitrary",) * n_axes, vmem_limit_bytes=VMEM_LIMIT)


def kernel(x, positions, w_norm_mix, w_in, b_gate, w_dw, b_dw, w_conv_norm, w_conv_out, q_norm_w, k_norm_w,
           w_attn_out, w_o, w_norm_mlp, w_mlp_in, w_mlp_out):
    batch, seq, d_model = x.shape
    depth = w_in.shape[0]
    assert d_model == D_MODEL and seq % ATTN_TILE == 0 and ATTN_TILE % TOKEN_TILE == 0
    n_tok = batch * seq
    tm = TOKEN_TILE
    n_tiles = n_tok // tm
    in_width = w_in.shape[-1]

    xf = x.reshape(n_tok, d_model)
    pos = jnp.broadcast_to(positions.reshape(n_tok, 1).astype(_F32), (n_tok, LANES))
    rope_rows = _rope_rows()
    head = jnp.arange(GROUP_WIDTH) // HEAD_DIM
    seg = (head[:, None] == head[None, :]).astype(_BF16)

    def row_spec(width):
        return pl.BlockSpec((tm, width), lambda i: (i, 0))

    for layer in range(depth):
        qk_gain = jnp.stack([jnp.tile(q_norm_w[layer], GROUP_WIDTH // HEAD_DIM),
                             jnp.tile(k_norm_w[layer], GROUP_WIDTH // HEAD_DIM)]).astype(_F32)

        def grp_spec(dil):
            return pl.BlockSpec((tm // dil, dil * GROUP_WIDTH), lambda i: (i, 0))

        grp_specs = [grp_spec(dil) for dil in DILATIONS] * 3
        grp_shapes = [jax.ShapeDtypeStruct((n_tok // dil, dil * GROUP_WIDTH), _BF16) for dil in DILATIONS] * 3
        outs = pl.pallas_call(
            _inproj_kernel,
            grid=(n_tiles,),
            in_specs=[row_spec(d_model), row_spec(LANES), _const_spec((1, d_model)),
                      _const_spec((d_model, in_width)), _const_spec((1, 2 * d_model)),
                      _const_spec((2, GROUP_WIDTH)), _const_spec((8, LANES)),
                      _const_spec((GROUP_WIDTH, GROUP_WIDTH))],
            out_specs=[row_spec(CONV_WIDTH)] + grp_specs + [row_spec(2 * d_model)],
            out_shape=[jax.ShapeDtypeStruct((n_tok, CONV_WIDTH), _BF16)] + grp_shapes
                      + [jax.ShapeDtypeStruct((n_tok, 2 * d_model), _BF16)],
            scratch_shapes=[pltpu.VMEM((GROUP_WIDTH // LANES, tm, LANES), _F32)],
            compiler_params=_params(1),
            name="inproj",
        )(xf, pos, w_norm_mix[layer].reshape(1, d_model), w_in[layer].astype(_BF16),
          b_gate[layer].reshape(1, 2 * d_model), qk_gain, rope_rows, seg)
        u, qkv, gates = outs[0], outs[1:10], outs[10]

        n_attn_tiles = seq // ATTN_TILE
        in_specs, operands, scratch = [], [], []
        for g, dil in enumerate(DILATIONS):
            rows, width = ATTN_TILE // dil, dil * GROUP_WIDTH
            spans_per_tile, spans_per_seq = rows // SPAN, seq // dil // SPAN
            cur = pl.BlockSpec((rows, width), lambda b, i: (b * n_attn_tiles + i, 0))
            prev = pl.BlockSpec(
                (SPAN, width),
                lambda b, i, n=spans_per_tile, s=spans_per_seq: (b * s + jnp.maximum(i * n - 1, 0), 0))
            in_specs += [cur, cur, prev, cur, prev]
            operands += [qkv[g], qkv[3 + g], qkv[3 + g], qkv[6 + g], qkv[6 + g]]
            scratch += [pltpu.VMEM((SPAN + rows, width), _BF16)] * 2
        scratch += [pltpu.VMEM((ATTN_WIDTH // LANES, ATTN_TILE, LANES), _F32)] * 2
        ow = pl.pallas_call(
            _attn_kernel,
            grid=(batch, n_attn_tiles),
            in_specs=in_specs,
            out_specs=pl.BlockSpec((ATTN_TILE, ATTN_WIDTH), lambda b, i: (b * n_attn_tiles + i, 0)),
            out_shape=jax.ShapeDtypeStruct((n_tok, ATTN_WIDTH), _BF16),
            scratch_shapes=scratch,
            compiler_params=_params(2),
            name="attention",
        )(*operands)

        halo_spec = pl.BlockSpec((CONV_HALO, CONV_WIDTH),
                                 lambda i: (jnp.maximum(i * (tm // CONV_HALO) - 1, 0), 0))
        x1 = pl.pallas_call(
            functools.partial(_mix_kernel, tiles_per_seq=seq // tm),
            grid=(n_tiles,),
            in_specs=[row_spec(d_model), row_spec(CONV_WIDTH), halo_spec, row_spec(ATTN_WIDTH),
                      row_spec(2 * d_model),
                      _const_spec((CONV_KERNEL, CONV_WIDTH)), _const_spec((1, CONV_WIDTH)),
                      _const_spec((1, CONV_WIDTH)), _const_spec((CONV_WIDTH, d_model)),
                      _const_spec((ATTN_WIDTH, d_model)), _const_spec((d_model, d_model))],
            out_specs=row_spec(d_model),
            out_shape=jax.ShapeDtypeStruct((n_tok, d_model), _F32),
            scratch_shapes=[pltpu.VMEM((CONV_WIDTH // LANES, CONV_HALO + tm, LANES), _F32),
                            pltpu.VMEM((CONV_WIDTH // LANES, tm, LANES), _F32)],
            compiler_params=_params(1),
            name="mixer_out",
        )(xf, u, u, ow, gates, w_dw[layer], b_dw[layer].reshape(1, CONV_WIDTH),
          w_conv_norm[layer].reshape(1, CONV_WIDTH), w_conv_out[layer].astype(_BF16),
          w_attn_out[layer].astype(_BF16), w_o[layer].astype(_BF16))

        xf = pl.pallas_call(
            _mlp_kernel,
            grid=(n_tiles,),
            in_specs=[row_spec(d_model), _const_spec((1, d_model)),
                      _const_spec((d_model, D_FF)), _const_spec((D_FF, d_model))],
            out_specs=row_spec(d_model),
            out_shape=jax.ShapeDtypeStruct((n_tok, d_model), _F32),
            compiler_params=_params(1),
            name="mlp",
        )(x1, w_norm_mlp[layer].reshape(1, d_model), w_mlp_in[layer].astype(_BF16),
          w_mlp_out[layer].astype(_BF16))

    return xf.reshape(batch, seq, d_model)
```

```python
import functools

import jax
import jax.numpy as jnp
from jax import lax
from jax.experimental import pallas as pl
from jax.experimental.pallas import tpu as pltpu

D_MODEL = 1024
HEAD_DIM = 64
HEADS_PER_GROUP = 4
GROUP_WIDTH = HEADS_PER_GROUP * HEAD_DIM
DILATIONS = (1, 4, 16)
SPAN = 128
N_GROUPS = len(DILATIONS)
ATTN_WIDTH = N_GROUPS * GROUP_WIDTH
ROPE_THETA = 500000.0
ROPE_DIM = HEAD_DIM // 4
CONV_WIDTH = D_MODEL
CONV_KERNEL = 31
D_FF = 4 * D_MODEL
NORM_EPS = 1e-6
MASK_VALUE = -1e30

LANES = 128
CONV_HALO = 32
ATTN_TILE = SPAN * DILATIONS[-1]
TOKEN_TILE = 512
CONV_ROWS = 32
CONV_STRIDE = 4
FF_CHUNK = 512
COMBINE_ROWS = 256
VMEM_LIMIT = 56 * 1024 * 1024

_F32 = jnp.float32
_BF16 = jnp.bfloat16


def _dot(a, b):
    return jnp.dot(a, b, preferred_element_type=_F32)


def _sigmoid(z):
    return 0.5 * jnp.tanh(0.5 * z) + 0.5


def _rms_scale(xf):
    return lax.rsqrt(jnp.mean(xf * xf, axis=-1, keepdims=True) + NORM_EPS)


def _inproj_kernel(x_ref, pos_ref, wn_ref, w_ref, bg_ref, qkw_ref, freq_ref, sel_ref, aux_ref, seg_ref,
                   u_ref, q1_ref, q2_ref, q3_ref, k1_ref, k2_ref, k3_ref, v1_ref, v2_ref, v3_ref, g_ref,
                   z_ref):
    tm = x_ref.shape[0]
    x = x_ref[...]
    n = (x * _rms_scale(x) * wn_ref[...]).astype(_BF16)

    ang = freq_ref[...] * pos_ref[0]
    cs = jnp.concatenate([jnp.cos(ang), jnp.sin(ang)], axis=0).T
    cs_hi = cs.astype(_BF16)
    cs_lo = (cs - cs_hi.astype(_F32)).astype(_BF16)
    tables = _dot(cs_hi, sel_ref[...]) + _dot(cs_lo, sel_ref[...])
    cos = tables[:, 0:LANES] + aux_ref[0:1, :]
    sin = tables[:, LANES:2 * LANES]
    first_half = aux_ref[1:2, :] > 0.5
    seg = seg_ref[...]

    def store_by_residue(slot, out_ref, dil):
        for t in range(GROUP_WIDTH // LANES):
            for r in range(dil):
                rows = z_ref[slot, t] if dil == 1 else z_ref[slot, t, pl.ds(r, tm // dil, stride=dil), :]
                c = r * GROUP_WIDTH + t * LANES
                out_ref[:, c:c + LANES] = rows.astype(_BF16)

    def finish_qk(z, slot, gain_row, scale, out_ref, dil):
        ss = _dot((z * z).astype(_BF16), seg)
        z = z * lax.rsqrt(ss * (1.0 / HEAD_DIM) + NORM_EPS) * qkw_ref[gain_row:gain_row + 1, :]
        for t in range(GROUP_WIDTH // LANES):
            zt = z[:, t * LANES:(t + 1) * LANES]
            partner = jnp.where(first_half, pltpu.roll(zt, LANES - ROPE_DIM // 2, 1),
                                pltpu.roll(zt, ROPE_DIM // 2, 1))
            z_ref[slot, t] = (zt * cos + partner * sin) * scale
        store_by_residue(slot, out_ref, dil)

    def finish_v(z, slot, out_ref, dil):
        for t in range(GROUP_WIDTH // LANES):
            z_ref[slot, t] = z[:, t * LANES:(t + 1) * LANES]
        store_by_residue(slot, out_ref, dil)

    def finish_gate(z, t):
        z = z + bg_ref[:, t * D_MODEL:(t + 1) * D_MODEL]
        g_ref[:, t * D_MODEL:(t + 1) * D_MODEL] = _sigmoid(z).astype(_BF16)

    def finish_glu(z):
        u_ref[...] = (z[:, 0:CONV_WIDTH] * _sigmoid(z[:, CONV_WIDTH:2 * CONV_WIDTH])).astype(_BF16)

    c0 = 2 * CONV_WIDTH
    c3 = c0 + 3 * ATTN_WIDTH
    q_refs = (q1_ref, q2_ref, q3_ref)
    k_refs = (k1_ref, k2_ref, k3_ref)
    v_refs = (v1_ref, v2_ref, v3_ref)

    segments = [(c3, D_MODEL, functools.partial(finish_gate, t=0)),
                (c3 + D_MODEL, D_MODEL, functools.partial(finish_gate, t=1))]
    for g, dil in enumerate(DILATIONS):
        segments.append((c0 + g * GROUP_WIDTH, GROUP_WIDTH,
                         functools.partial(finish_qk, slot=0, gain_row=0, scale=HEAD_DIM ** -0.5,
                                           out_ref=q_refs[g], dil=dil)))
        segments.append((c0 + ATTN_WIDTH + g * GROUP_WIDTH, GROUP_WIDTH,
                         functools.partial(finish_qk, slot=1, gain_row=1, scale=1.0,
                                           out_ref=k_refs[g], dil=dil)))
    segments.append((0, 2 * CONV_WIDTH, finish_glu))
    for g, dil in enumerate(DILATIONS):
        segments.append((c0 + 2 * ATTN_WIDTH + g * GROUP_WIDTH, GROUP_WIDTH,
                         functools.partial(finish_v, slot=g % 2, out_ref=v_refs[g], dil=dil)))

    pending = None
    for col, width, finish in segments:
        z = _dot(n, w_ref[:, col:col + width])
        if pending is not None:
            pending()
        pending = functools.partial(finish, z)
    pending()


def _attn_kernel(q1_ref, k1_ref, k1p_ref, v1_ref, v1p_ref,
                 q2_ref, k2_ref, k2p_ref, v2_ref, v2p_ref,
                 q3_ref, k3_ref, k3p_ref, v3_ref, v3p_ref,
                 ow_ref,
                 kx1, vx1, kx2, vx2, kx3, vx3, o_scr, l_scr):
    tile = pl.program_id(1)

    for kx, vx, kp, kc, vp, vc in ((kx1, vx1, k1p_ref, k1_ref, v1p_ref, v1_ref),
                                   (kx2, vx2, k2p_ref, k2_ref, v2p_ref, v2_ref),
                                   (kx3, vx3, k3p_ref, k3_ref, v3p_ref, v3_ref)):
        kx[0:SPAN, :] = kp[...]
        kx[SPAN:, :] = kc[...]
        vx[0:SPAN, :] = vp[...]
        vx[SPAN:, :] = vc[...]

    qi = lax.broadcasted_iota(jnp.int32, (SPAN, 2 * SPAN), 0)
    ki = lax.broadcasted_iota(jnp.int32, (SPAN, 2 * SPAN), 1)
    dist = qi + SPAN - ki
    band = jnp.where((dist >= 0) & (dist <= SPAN), 0.0, MASK_VALUE).astype(_F32)
    band_first = jnp.where((ki >= SPAN) | (tile > 0), band, MASK_VALUE)

    lane = lax.broadcasted_iota(jnp.int32, (SPAN, LANES), 1)
    low_head = lane < HEAD_DIM

    def unit(q_ref, kx, vx, row0, col0, bias, tok_rows, out_tile0):
        for pair in range(GROUP_WIDTH // LANES):
            c = col0 + pair * LANES
            q = q_ref[pl.ds(row0, SPAN), c:c + LANES]
            kk = kx[pl.ds(row0, 2 * SPAN), c:c + LANES]
            vv = vx[pl.ds(row0, 2 * SPAN), c:c + LANES]
            outs, lses = [], []
            for keep in (low_head, ~low_head):
                qh = jnp.where(keep, q, jnp.zeros_like(q))
                s = lax.dot_general(qh, kk, (((1,), (1,)), ((), ())), preferred_element_type=_F32) + bias
                m = jnp.max(s, axis=-1, keepdims=True)
                e = jnp.exp(s - m)
                l = jnp.sum(e, axis=-1, keepdims=True)
                outs.append(_dot(e.astype(_BF16), vv) * (1.0 / l))
                lses.append(jnp.broadcast_to(m + jnp.log(l), (SPAN, LANES)))
            ot = out_tile0 + pair
            o_scr[ot, tok_rows, :] = jnp.where(low_head, outs[0], outs[1])
            l_scr[ot, tok_rows, :] = jnp.where(low_head, lses[0], lses[1])

    unit(q1_ref, kx1, vx1, 0, 0, band_first, pl.ds(0, SPAN), 0)

    def g1_body(j, carry):
        row0 = pl.multiple_of(j * SPAN, SPAN)
        unit(q1_ref, kx1, vx1, row0, 0, band, pl.ds(row0, SPAN), 0)
        return carry

    lax.fori_loop(1, ATTN_TILE // SPAN, g1_body, 0)

    for g in (1, 2):
        dil = DILATIONS[g]
        q_ref, kx, vx = ((q2_ref, kx2, vx2), (q3_ref, kx3, vx3))[g - 1]
        for r in range(dil):
            for j in range(ATTN_TILE // dil // SPAN):
                unit(q_ref, kx, vx, j * SPAN, r * GROUP_WIDTH, band_first if j == 0 else band,
                     pl.ds(dil * SPAN * j + r, SPAN, stride=dil), g * (GROUP_WIDTH // LANES))

    def combine(i, carry):
        rows = pl.ds(pl.multiple_of(i * COMBINE_ROWS, COMBINE_ROWS), COMBINE_ROWS)
        tiles_per_group = GROUP_WIDTH // LANES
        for t in range(tiles_per_group):
            ls = [l_scr[g * tiles_per_group + t, rows, :] for g in range(N_GROUPS)]
            mx = jnp.maximum(jnp.maximum(ls[0], ls[1]), ls[2])
            es = [jnp.exp(l - mx) for l in ls]
            inv = 1.0 / (es[0] + es[1] + es[2])
            for g in range(N_GROUPS):
                ot = g * tiles_per_group + t
                ow_ref[rows, ot * LANES:(ot + 1) * LANES] = (o_scr[ot, rows, :] * (es[g] * inv)).astype(_BF16)
        return carry

    lax.fori_loop(0, ATTN_TILE // COMBINE_ROWS, combine, 0)


def _mix_kernel(x_ref, u_ref, uh_ref, ow_ref, g_ref,
                wdw_ref, bdw_ref, wcn_ref, wco_ref, wao_ref, wo_ref,
                x1_ref, ext_ref, y_ref, *, tiles_per_seq):
    first = (pl.program_id(0) % tiles_per_seq) == 0
    tm = x_ref.shape[0]

    shift = CONV_HALO - (CONV_KERNEL - 1)
    block = CONV_ROWS * CONV_STRIDE
    for c in range(CONV_WIDTH // LANES):
        cols = slice(c * LANES, (c + 1) * LANES)
        halo = uh_ref[:, cols].astype(_F32)
        ext_ref[c, 0:CONV_HALO, :] = jnp.where(first, jnp.zeros_like(halo), halo)
        ext_ref[c, CONV_HALO:, :] = u_ref[:, cols].astype(_F32)

        taps = [jnp.broadcast_to(wdw_ref[k:k + 1, cols], (CONV_ROWS, LANES)) for k in range(CONV_KERNEL)]
        bias = jnp.broadcast_to(bdw_ref[:, cols], (CONV_ROWS, LANES))

        for base in range(0, tm, block):
            accs = [bias] * CONV_STRIDE
            for s in range(CONV_KERNEL + CONV_STRIDE - 1):
                rows = ext_ref[c, pl.ds(base + shift + s, CONV_ROWS, stride=CONV_STRIDE), :]
                for phase in range(CONV_STRIDE):
                    if 0 <= s - phase < CONV_KERNEL:
                        accs[phase] = accs[phase] + rows * taps[s - phase]
            for phase in range(CONV_STRIDE):
                y_ref[c, pl.ds(base + phase, CONV_ROWS, stride=CONV_STRIDE), :] = accs[phase]

    y = jnp.concatenate([y_ref[c] for c in range(CONV_WIDTH // LANES)], axis=1)
    y = y * _rms_scale(y) * wcn_ref[...]
    y = y * _sigmoid(y)
    branch_a = _dot(y.astype(_BF16), wco_ref[...])

    branch_b = _dot(ow_ref[...], wao_ref[...])

    merged = (g_ref[:, 0:D_MODEL].astype(_F32) * branch_a
              + g_ref[:, D_MODEL:2 * D_MODEL].astype(_F32) * branch_b)
    x1_ref[...] = x_ref[...] + _dot(merged.astype(_BF16), wo_ref[...])


def _mlp_kernel(x_ref, wn_ref, w1_ref, w2_ref, out_ref):
    x = x_ref[...]
    n = (x * _rms_scale(x) * wn_ref[...]).astype(_BF16)
    acc = x
    for c in range(D_FF // FF_CHUNK):
        h = _dot(n, w1_ref[:, c * FF_CHUNK:(c + 1) * FF_CHUNK])
        h = jnp.maximum(h, 0.0)
        acc = acc + _dot((h * h).astype(_BF16), w2_ref[c * FF_CHUNK:(c + 1) * FF_CHUNK, :])
    out_ref[...] = acc


def _const_spec(shape):
    return pl.BlockSpec(shape, lambda *_: (0,) * len(shape))


def _rope_constants(tm):
    half = ROPE_DIM // 2
    inv_freq = ROPE_THETA ** (-jnp.arange(0, ROPE_DIM, 2, dtype=_F32) / ROPE_DIM)
    freq = jnp.broadcast_to(inv_freq[:, None], (half, tm))
    d = jnp.arange(LANES) % HEAD_DIM
    rotary = d < ROPE_DIM
    hit = (rotary[None, :] & ((d % half)[None, :] == jnp.arange(half)[:, None])).astype(_F32)
    sign = jnp.where(d < half, -1.0, 1.0)[None, :]
    zero = jnp.zeros_like(hit)
    sel = jnp.concatenate([jnp.concatenate([hit, zero], axis=1),
                           jnp.concatenate([zero, hit * sign], axis=1)], axis=0).astype(_BF16)
    aux = jnp.stack([1.0 - rotary.astype(_F32), (d < half).astype(_F32)]
                    + [jnp.zeros((LANES,), _F32)] * 6)
    return freq, sel, aux


def _params(n_axes):
    return pltpu.CompilerParams(dimension_semantics=("arbitrary",) * n_axes, vmem_limit_bytes=VMEM_LIMIT)


def kernel(x, positions, w_norm_mix, w_in, b_gate, w_dw, b_dw, w_conv_norm, w_conv_out, q_norm_w, k_norm_w,
           w_attn_out, w_o, w_norm_mlp, w_mlp_in, w_mlp_out):
    batch, seq, d_model = x.shape
    depth = w_in.shape[0]
    assert d_model == D_MODEL and seq % ATTN_TILE == 0 and ATTN_TILE % TOKEN_TILE == 0
    n_tok = batch * seq
    tm = TOKEN_TILE
    n_tiles = n_tok // tm
    in_width = w_in.shape[-1]

    xf = x.reshape(n_tok, d_model)
    pos = positions.astype(_F32).reshape(n_tiles, 1, tm)
    freq, sel, aux = _rope_constants(tm)
    head = jnp.arange(GROUP_WIDTH) // HEAD_DIM
    seg = (head[:, None] == head[None, :]).astype(_BF16)

    def row_spec(width):
        return pl.BlockSpec((tm, width), lambda i: (i, 0))

    for layer in range(depth):
        qk_gain = jnp.stack([jnp.tile(q_norm_w[layer], GROUP_WIDTH // HEAD_DIM),
                             jnp.tile(k_norm_w[layer], GROUP_WIDTH // HEAD_DIM)]).astype(_F32)

        def grp_spec(dil):
            return pl.BlockSpec((tm // dil, dil * GROUP_WIDTH), lambda i: (i, 0))

        grp_specs = [grp_spec(dil) for dil in DILATIONS] * 3
        grp_shapes = [jax.ShapeDtypeStruct((n_tok // dil, dil * GROUP_WIDTH), _BF16) for dil in DILATIONS] * 3
        outs = pl.pallas_call(
            _inproj_kernel,
            grid=(n_tiles,),
            in_specs=[row_spec(d_model), pl.BlockSpec((1, 1, tm), lambda i: (i, 0, 0)),
                      _const_spec((1, d_model)),
                      _const_spec((d_model, in_width)), _const_spec((1, 2 * d_model)),
                      _const_spec((2, GROUP_WIDTH)), _const_spec((ROPE_DIM // 2, tm)),
                      _const_spec((ROPE_DIM, 2 * LANES)), _const_spec((8, LANES)),
                      _const_spec((GROUP_WIDTH, GROUP_WIDTH))],
            out_specs=[row_spec(CONV_WIDTH)] + grp_specs + [row_spec(2 * d_model)],
            out_shape=[jax.ShapeDtypeStruct((n_tok, CONV_WIDTH), _BF16)] + grp_shapes
                      + [jax.ShapeDtypeStruct((n_tok, 2 * d_model), _BF16)],
            scratch_shapes=[pltpu.VMEM((2, GROUP_WIDTH // LANES, tm, LANES), _F32)],
            compiler_params=_params(1),
            name="inproj",
        )(xf, pos, w_norm_mix[layer].reshape(1, d_model), w_in[layer].astype(_BF16),
          b_gate[layer].reshape(1, 2 * d_model), qk_gain, freq, sel, aux, seg)
        u, qkv, gates = outs[0], outs[1:10], outs[10]

        n_attn_tiles = seq // ATTN_TILE
        in_specs, operands, scratch = [], [], []
        for g, dil in enumerate(DILATIONS):
            rows, width = ATTN_TILE // dil, dil * GROUP_WIDTH
            spans_per_tile, spans_per_seq = rows // SPAN, seq // dil // SPAN
            cur = pl.BlockSpec((rows, width), lambda b, i: (b * n_attn_tiles + i, 0))
            prev = pl.BlockSpec(
                (SPAN, width),
                lambda b, i, n=spans_per_tile, s=spans_per_seq: (b * s + jnp.maximum(i * n - 1, 0), 0))
            in_specs += [cur, cur, prev, cur, prev]
            operands += [qkv[g], qkv[3 + g], qkv[3 + g], qkv[6 + g], qkv[6 + g]]
            scratch += [pltpu.VMEM((SPAN + rows, width), _BF16)] * 2
        scratch += [pltpu.VMEM((ATTN_WIDTH // LANES, ATTN_TILE, LANES), _F32)] * 2
        ow = pl.pallas_call(
            _attn_kernel,
            grid=(batch, n_attn_tiles),
            in_specs=in_specs,
            out_specs=pl.BlockSpec((ATTN_TILE, ATTN_WIDTH), lambda b, i: (b * n_attn_tiles + i, 0)),
            out_shape=jax.ShapeDtypeStruct((n_tok, ATTN_WIDTH), _BF16),
            scratch_shapes=scratch,
            compiler_params=_params(2),
            name="attention",
        )(*operands)

        halo_spec = pl.BlockSpec((CONV_HALO, CONV_WIDTH),
                                 lambda i: (jnp.maximum(i * (tm // CONV_HALO) - 1, 0), 0))
        x1 = pl.pallas_call(
            functools.partial(_mix_kernel, tiles_per_seq=seq // tm),
            grid=(n_tiles,),
            in_specs=[row_spec(d_model), row_spec(CONV_WIDTH), halo_spec, row_spec(ATTN_WIDTH),
                      row_spec(2 * d_model),
                      _const_spec((CONV_KERNEL, CONV_WIDTH)), _const_spec((1, CONV_WIDTH)),
                      _const_spec((1, CONV_WIDTH)), _const_spec((CONV_WIDTH, d_model)),
                      _const_spec((ATTN_WIDTH, d_model)), _const_spec((d_model, d_model))],
            out_specs=row_spec(d_model),
            out_shape=jax.ShapeDtypeStruct((n_tok, d_model), _F32),
            scratch_shapes=[pltpu.VMEM((CONV_WIDTH // LANES, CONV_HALO + tm, LANES), _F32),
                            pltpu.VMEM((CONV_WIDTH // LANES, tm, LANES), _F32)],
            compiler_params=_params(1),
            name="mixer_out",
        )(xf, u, u, ow, gates, w_dw[layer], b_dw[layer].reshape(1, CONV_WIDTH),
          w_conv_norm[layer].reshape(1, CONV_WIDTH), w_conv_out[layer].astype(_BF16),
          w_attn_out[layer].astype(_BF16), w_o[layer].astype(_BF16))

        xf = pl.pallas_call(
            _mlp_kernel,
            grid=(n_tiles,),
            in_specs=[row_spec(d_model), _const_spec((1, d_model)),
                      _const_spec((d_model, D_FF)), _const_spec((D_FF, d_model))],
            out_specs=row_spec(d_model),
            out_shape=jax.ShapeDtypeStruct((n_tok, d_model), _F32),
            compiler_params=_params(1),
            name="mlp",
        )(x1, w_norm_mlp[layer].reshape(1, d_model), w_mlp_in[layer].astype(_BF16),
          w_mlp_out[layer].astype(_BF16))

    return xf.reshape(batch, seq, d_model)
```

```python
import functools

import jax
import jax.numpy as jnp
from jax import lax
from jax.experimental import pallas as pl
from jax.experimental.pallas import tpu as pltpu

D_MODEL = 1024
HEAD_DIM = 64
HEADS_PER_GROUP = 4
GROUP_WIDTH = HEADS_PER_GROUP * HEAD_DIM
DILATIONS = (1, 4, 16)
SPAN = 128
N_GROUPS = len(DILATIONS)
ATTN_WIDTH = N_GROUPS * GROUP_WIDTH
ROPE_THETA = 500000.0
ROPE_DIM = HEAD_DIM // 4
CONV_WIDTH = D_MODEL
CONV_KERNEL = 31
D_FF = 4 * D_MODEL
NORM_EPS = 1e-6
MASK_VALUE = -1e30

LANES = 128
CONV_HALO = 32
ATTN_TILE = SPAN * DILATIONS[-1]
TOKEN_TILE = 512
CONV_ROWS = 32
CONV_STRIDE = 4
FF_CHUNK = 2048
COMBINE_ROWS = 256
VMEM_LIMIT = 56 * 1024 * 1024

_F32 = jnp.float32
_BF16 = jnp.bfloat16


def _dot(a, b):
    return jnp.dot(a, b, preferred_element_type=_F32)


def _sigmoid(z):
    return 0.5 * jnp.tanh(0.5 * z) + 0.5


def _rms_scale(xf):
    return lax.rsqrt(jnp.mean(xf * xf, axis=-1, keepdims=True) + NORM_EPS)


def _inproj_kernel(x_ref, pos_ref, wn_ref, w_ref, bg_ref, qkw_ref, freq_ref, sel_ref, aux_ref, seg_ref,
                   u_ref, q1_ref, q2_ref, q3_ref, k1_ref, k2_ref, k3_ref, v1_ref, v2_ref, v3_ref, g_ref,
                   z_ref):
    tm = x_ref.shape[0]
    x = x_ref[...]
    n = (x * _rms_scale(x) * wn_ref[...]).astype(_BF16)

    ang = freq_ref[...] * pos_ref[0]
    cs = jnp.concatenate([jnp.cos(ang), jnp.sin(ang)], axis=0).T
    cs_hi = cs.astype(_BF16)
    cs_lo = (cs - cs_hi.astype(_F32)).astype(_BF16)
    tables = _dot(cs_hi, sel_ref[...]) + _dot(cs_lo, sel_ref[...])
    cos = tables[:, 0:LANES] + aux_ref[0:1, :]
    sin = tables[:, LANES:2 * LANES]
    first_half = aux_ref[1:2, :] > 0.5
    seg = seg_ref[...]

    def store_by_residue(slot, out_ref, dil):
        for t in range(GROUP_WIDTH // LANES):
            for r in range(dil):
                rows = z_ref[slot, t] if dil == 1 else z_ref[slot, t, pl.ds(r, tm // dil, stride=dil), :]
                c = r * GROUP_WIDTH + t * LANES
                out_ref[:, c:c + LANES] = rows.astype(_BF16)

    def finish_qk(z, slot, gain_row, scale, out_ref, dil):
        ss = _dot((z * z).astype(_BF16), seg)
        z = z * lax.rsqrt(ss * (1.0 / HEAD_DIM) + NORM_EPS) * qkw_ref[gain_row:gain_row + 1, :]
        for t in range(GROUP_WIDTH // LANES):
            zt = z[:, t * LANES:(t + 1) * LANES]
            partner = jnp.where(first_half, pltpu.roll(zt, LANES - ROPE_DIM // 2, 1),
                                pltpu.roll(zt, ROPE_DIM // 2, 1))
            z_ref[slot, t] = (zt * cos + partner * sin) * scale
        store_by_residue(slot, out_ref, dil)

    def finish_v(z, slot, out_ref, dil):
        for t in range(GROUP_WIDTH // LANES):
            z_ref[slot, t] = z[:, t * LANES:(t + 1) * LANES]
        store_by_residue(slot, out_ref, dil)

    def finish_gate(z, t):
        z = z + bg_ref[:, t * D_MODEL:(t + 1) * D_MODEL]
        g_ref[:, t * D_MODEL:(t + 1) * D_MODEL] = _sigmoid(z).astype(_BF16)

    def finish_glu(z):
        u_ref[...] = (z[:, 0:CONV_WIDTH] * _sigmoid(z[:, CONV_WIDTH:2 * CONV_WIDTH])).astype(_BF16)

    c0 = 2 * CONV_WIDTH
    c3 = c0 + 3 * ATTN_WIDTH
    q_refs = (q1_ref, q2_ref, q3_ref)
    k_refs = (k1_ref, k2_ref, k3_ref)
    v_refs = (v1_ref, v2_ref, v3_ref)

    segments = [(c3, D_MODEL, functools.partial(finish_gate, t=0)),
                (c3 + D_MODEL, D_MODEL, functools.partial(finish_gate, t=1))]
    for g, dil in enumerate(DILATIONS):
        segments.append((c0 + g * GROUP_WIDTH, GROUP_WIDTH,
                         functools.partial(finish_qk, slot=0, gain_row=0, scale=HEAD_DIM ** -0.5,
                                           out_ref=q_refs[g], dil=dil)))
        segments.append((c0 + ATTN_WIDTH + g * GROUP_WIDTH, GROUP_WIDTH,
                         functools.partial(finish_qk, slot=1, gain_row=1, scale=1.0,
                                           out_ref=k_refs[g], dil=dil)))
    segments.append((0, 2 * CONV_WIDTH, finish_glu))
    for g, dil in enumerate(DILATIONS):
        segments.append((c0 + 2 * ATTN_WIDTH + g * GROUP_WIDTH, GROUP_WIDTH,
                         functools.partial(finish_v, slot=g % 2, out_ref=v_refs[g], dil=dil)))

    pending = None
    for col, width, finish in segments:
        z = _dot(n, w_ref[:, col:col + width])
        if pending is not None:
            pending()
        pending = functools.partial(finish, z)
    pending()


def _attn_kernel(q1_ref, k1_ref, k1p_ref, v1_ref, v1p_ref,
                 q2_ref, k2_ref, k2p_ref, v2_ref, v2p_ref,
                 q3_ref, k3_ref, k3p_ref, v3_ref, v3p_ref,
                 ow_ref,
                 kx1, vx1, kx2, vx2, kx3, vx3, o_scr, l_scr):
    tile = pl.program_id(1)

    for kx, vx, kp, kc, vp, vc in ((kx1, vx1, k1p_ref, k1_ref, v1p_ref, v1_ref),
                                   (kx2, vx2, k2p_ref, k2_ref, v2p_ref, v2_ref),
                                   (kx3, vx3, k3p_ref, k3_ref, v3p_ref, v3_ref)):
        kx[0:SPAN, :] = kp[...]
        kx[SPAN:, :] = kc[...]
        vx[0:SPAN, :] = vp[...]
        vx[SPAN:, :] = vc[...]

    qi = lax.broadcasted_iota(jnp.int32, (SPAN, 2 * SPAN), 0)
    ki = lax.broadcasted_iota(jnp.int32, (SPAN, 2 * SPAN), 1)
    dist = qi + SPAN - ki
    band = jnp.where((dist >= 0) & (dist <= SPAN), 0.0, MASK_VALUE).astype(_F32)
    band_first = jnp.where((ki >= SPAN) | (tile > 0), band, MASK_VALUE)

    lane = lax.broadcasted_iota(jnp.int32, (SPAN, LANES), 1)
    low_head = lane < HEAD_DIM

    def unit(q_ref, kx, vx, row0, col0, bias, tok_rows, out_tile0):
        for pair in range(GROUP_WIDTH // LANES):
            c = col0 + pair * LANES
            q = q_ref[pl.ds(row0, SPAN), c:c + LANES]
            kk = kx[pl.ds(row0, 2 * SPAN), c:c + LANES]
            vv = vx[pl.ds(row0, 2 * SPAN), c:c + LANES]
            outs, lses = [], []
            for keep in (low_head, ~low_head):
                qh = jnp.where(keep, q, jnp.zeros_like(q))
                s = lax.dot_general(qh, kk, (((1,), (1,)), ((), ())), preferred_element_type=_F32) + bias
                m = jnp.max(s, axis=-1, keepdims=True)
                e = jnp.exp(s - m)
                l = jnp.sum(e, axis=-1, keepdims=True)
                outs.append(_dot(e.astype(_BF16), vv) * (1.0 / l))
                lses.append(jnp.broadcast_to(m + jnp.log(l), (SPAN, LANES)))
            ot = out_tile0 + pair
            o_scr[ot, tok_rows, :] = jnp.where(low_head, outs[0], outs[1])
            l_scr[ot, tok_rows, :] = jnp.where(low_head, lses[0], lses[1])

    unit(q1_ref, kx1, vx1, 0, 0, band_first, pl.ds(0, SPAN), 0)

    def g1_body(j, carry):
        row0 = pl.multiple_of(j * SPAN, SPAN)
        unit(q1_ref, kx1, vx1, row0, 0, band, pl.ds(row0, SPAN), 0)
        return carry

    lax.fori_loop(1, ATTN_TILE // SPAN, g1_body, 0)

    for g in (1, 2):
        dil = DILATIONS[g]
        q_ref, kx, vx = ((q2_ref, kx2, vx2), (q3_ref, kx3, vx3))[g - 1]
        for r in range(dil):
            for j in range(ATTN_TILE // dil // SPAN):
                unit(q_ref, kx, vx, j * SPAN, r * GROUP_WIDTH, band_first if j == 0 else band,
                     pl.ds(dil * SPAN * j + r, SPAN, stride=dil), g * (GROUP_WIDTH // LANES))

    def combine(i, carry):
        rows = pl.ds(pl.multiple_of(i * COMBINE_ROWS, COMBINE_ROWS), COMBINE_ROWS)
        tiles_per_group = GROUP_WIDTH // LANES
        for t in range(tiles_per_group):
            ls = [l_scr[g * tiles_per_group + t, rows, :] for g in range(N_GROUPS)]
            mx = jnp.maximum(jnp.maximum(ls[0], ls[1]), ls[2])
            es = [jnp.exp(l - mx) for l in ls]
            inv = 1.0 / (es[0] + es[1] + es[2])
            for g in range(N_GROUPS):
                ot = g * tiles_per_group + t
                ow_ref[rows, ot * LANES:(ot + 1) * LANES] = (o_scr[ot, rows, :] * (es[g] * inv)).astype(_BF16)
        return carry

    lax.fori_loop(0, ATTN_TILE // COMBINE_ROWS, combine, 0)


def _mix_mlp_kernel(x_ref, u_ref, uh_ref, ow_ref, g_ref,
                    wdw_ref, bdw_ref, wcn_ref, wco_ref, wao_ref, wo_ref, wnm_ref, w1_ref, w2_ref,
                    out_ref, ext_ref, y_ref, x1_ref, hn_ref, *, tiles_per_seq, n_tiles):
    step = pl.program_id(0)
    tile = jnp.minimum(step, n_tiles - 1)
    first = (tile % tiles_per_seq) == 0
    tm = x_ref.shape[0]
    cur = step % 2
    prev = 1 - cur

    @pl.when(step == 0)
    def _():
        x1_ref[1] = jnp.zeros((tm, D_MODEL), _F32)

    shift = CONV_HALO - (CONV_KERNEL - 1)
    block = CONV_ROWS * CONV_STRIDE
    n_lane_tiles = CONV_WIDTH // LANES
    n_chunks = D_FF // FF_CHUNK
    lane_tiles_per_chunk = n_lane_tiles // n_chunks
    assert lane_tiles_per_chunk * n_chunks == n_lane_tiles

    for c in range(n_lane_tiles):
        cols = slice(c * LANES, (c + 1) * LANES)
        halo = uh_ref[:, cols].astype(_F32)
        ext_ref[c, 0:CONV_HALO, :] = jnp.where(first, jnp.zeros_like(halo), halo)
        ext_ref[c, CONV_HALO:, :] = u_ref[:, cols].astype(_F32)

    def conv_lane_tile(c):
        taps = [jnp.broadcast_to(wdw_ref[c, k:k + 1, :], (CONV_ROWS, LANES)) for k in range(CONV_KERNEL)]
        bias = jnp.broadcast_to(bdw_ref[c], (CONV_ROWS, LANES))

        for base in range(0, tm, block):
            accs = [bias] * CONV_STRIDE
            for s in range(CONV_KERNEL + CONV_STRIDE - 1):
                rows = ext_ref[c, pl.ds(base + shift + s, CONV_ROWS, stride=CONV_STRIDE), :]
                for phase in range(CONV_STRIDE):
                    if 0 <= s - phase < CONV_KERNEL:
                        accs[phase] = accs[phase] + rows * taps[s - phase]
            for phase in range(CONV_STRIDE):
                y_ref[c, pl.ds(base + phase, CONV_ROWS, stride=CONV_STRIDE), :] = accs[phase]

    xp = x1_ref[prev]
    hn_ref[...] = (xp * _rms_scale(xp) * wnm_ref[...]).astype(_BF16)
    out_ref[...] = xp

    def chunk_body(c, carry):
        h = jnp.maximum(_dot(hn_ref[...], w1_ref[c]), 0.0)
        rows = pl.ds(pl.multiple_of(c * FF_CHUNK, FF_CHUNK), FF_CHUNK)
        out_ref[...] += _dot((h * h).astype(_BF16), w2_ref[rows, :])
        for t in range(lane_tiles_per_chunk):
            conv_lane_tile(c * lane_tiles_per_chunk + t)
        return carry

    lax.fori_loop(0, n_chunks, chunk_body, 0)

    y = jnp.concatenate([y_ref[c] for c in range(n_lane_tiles)], axis=1)
    y = y * _rms_scale(y) * wcn_ref[...]
    y = y * _sigmoid(y)
    branch_a = _dot(y.astype(_BF16), wco_ref[...])
    branch_b = _dot(ow_ref[...], wao_ref[...])
    merged = (g_ref[:, 0:D_MODEL].astype(_F32) * branch_a
              + g_ref[:, D_MODEL:2 * D_MODEL].astype(_F32) * branch_b)
    x1_ref[cur] = x_ref[...] + _dot(merged.astype(_BF16), wo_ref[...])


def _const_spec(shape):
    return pl.BlockSpec(shape, lambda *_: (0,) * len(shape))


def _weight_spec(shape):
    return pl.BlockSpec(shape, lambda *_: (0,) * len(shape), pipeline_mode=pl.Buffered(1))


def _rope_constants(tm):
    half = ROPE_DIM // 2
    inv_freq = ROPE_THETA ** (-jnp.arange(0, ROPE_DIM, 2, dtype=_F32) / ROPE_DIM)
    freq = jnp.broadcast_to(inv_freq[:, None], (half, tm))
    d = jnp.arange(LANES) % HEAD_DIM
    rotary = d < ROPE_DIM
    hit = (rotary[None, :] & ((d % half)[None, :] == jnp.arange(half)[:, None])).astype(_F32)
    sign = jnp.where(d < half, -1.0, 1.0)[None, :]
    zero = jnp.zeros_like(hit)
    sel = jnp.concatenate([jnp.concatenate([hit, zero], axis=1),
                           jnp.concatenate([zero, hit * sign], axis=1)], axis=0).astype(_BF16)
    aux = jnp.stack([1.0 - rotary.astype(_F32), (d < half).astype(_F32)]
                    + [jnp.zeros((LANES,), _F32)] * 6)
    return freq, sel, aux


def _params(n_axes):
    return pltpu.CompilerParams(dimension_semantics=("arbitrary",) * n_axes, vmem_limit_bytes=VMEM_LIMIT)


def kernel(x, positions, w_norm_mix, w_in, b_gate, w_dw, b_dw, w_conv_norm, w_conv_out, q_norm_w, k_norm_w,
           w_attn_out, w_o, w_norm_mlp, w_mlp_in, w_mlp_out):
    batch, seq, d_model = x.shape
    depth = w_in.shape[0]
    assert d_model == D_MODEL and seq % ATTN_TILE == 0 and ATTN_TILE % TOKEN_TILE == 0
    n_tok = batch * seq
    tm = TOKEN_TILE
    n_tiles = n_tok // tm
    in_width = w_in.shape[-1]

    xf = x.reshape(n_tok, d_model)
    pos = positions.astype(_F32).reshape(n_tiles, 1, tm)
    freq, sel, aux = _rope_constants(tm)
    head = jnp.arange(GROUP_WIDTH) // HEAD_DIM
    seg = (head[:, None] == head[None, :]).astype(_BF16)

    def row_spec(width):
        return pl.BlockSpec((tm, width), lambda i: (i, 0))

    for layer in range(depth):
        qk_gain = jnp.stack([jnp.tile(q_norm_w[layer], GROUP_WIDTH // HEAD_DIM),
                             jnp.tile(k_norm_w[layer], GROUP_WIDTH // HEAD_DIM)]).astype(_F32)

        def grp_spec(dil):
            return pl.BlockSpec((tm // dil, dil * GROUP_WIDTH), lambda i: (i, 0))

        grp_specs = [grp_spec(dil) for dil in DILATIONS] * 3
        grp_shapes = [jax.ShapeDtypeStruct((n_tok // dil, dil * GROUP_WIDTH), _BF16) for dil in DILATIONS] * 3
        outs = pl.pallas_call(
            _inproj_kernel,
            grid=(n_tiles,),
            in_specs=[row_spec(d_model), pl.BlockSpec((1, 1, tm), lambda i: (i, 0, 0)),
                      _const_spec((1, d_model)),
                      _const_spec((d_model, in_width)), _const_spec((1, 2 * d_model)),
                      _const_spec((2, GROUP_WIDTH)), _const_spec((ROPE_DIM // 2, tm)),
                      _const_spec((ROPE_DIM, 2 * LANES)), _const_spec((8, LANES)),
                      _const_spec((GROUP_WIDTH, GROUP_WIDTH))],
            out_specs=[row_spec(CONV_WIDTH)] + grp_specs + [row_spec(2 * d_model)],
            out_shape=[jax.ShapeDtypeStruct((n_tok, CONV_WIDTH), _BF16)] + grp_shapes
                      + [jax.ShapeDtypeStruct((n_tok, 2 * d_model), _BF16)],
            scratch_shapes=[pltpu.VMEM((2, GROUP_WIDTH // LANES, tm, LANES), _F32)],
            compiler_params=_params(1),
            name="inproj",
        )(xf, pos, w_norm_mix[layer].reshape(1, d_model), w_in[layer].astype(_BF16),
          b_gate[layer].reshape(1, 2 * d_model), qk_gain, freq, sel, aux, seg)
        u, qkv, gates = outs[0], outs[1:10], outs[10]

        n_attn_tiles = seq // ATTN_TILE
        in_specs, operands, scratch = [], [], []
        for g, dil in enumerate(DILATIONS):
            rows, width = ATTN_TILE // dil, dil * GROUP_WIDTH
            spans_per_tile, spans_per_seq = rows // SPAN, seq // dil // SPAN
            cur = pl.BlockSpec((rows, width), lambda b, i: (b * n_attn_tiles + i, 0))
            prev = pl.BlockSpec(
                (SPAN, width),
                lambda b, i, n=spans_per_tile, s=spans_per_seq: (b * s + jnp.maximum(i * n - 1, 0), 0))
            in_specs += [cur, cur, prev, cur, prev]
            operands += [qkv[g], qkv[3 + g], qkv[3 + g], qkv[6 + g], qkv[6 + g]]
            scratch += [pltpu.VMEM((SPAN + rows, width), _BF16)] * 2
        scratch += [pltpu.VMEM((ATTN_WIDTH // LANES, ATTN_TILE, LANES), _F32)] * 2
        ow = pl.pallas_call(
            _attn_kernel,
            grid=(batch, n_attn_tiles),
            in_specs=in_specs,
            out_specs=pl.BlockSpec((ATTN_TILE, ATTN_WIDTH), lambda b, i: (b * n_attn_tiles + i, 0)),
            out_shape=jax.ShapeDtypeStruct((n_tok, ATTN_WIDTH), _BF16),
            scratch_shapes=scratch,
            compiler_params=_params(2),
            name="attention",
        )(*operands)

        n_lane_tiles, n_chunks = CONV_WIDTH // LANES, D_FF // FF_CHUNK

        def tile_spec(width):
            return pl.BlockSpec((tm, width), lambda i: (jnp.minimum(i, n_tiles - 1), 0))

        halo_spec = pl.BlockSpec(
            (CONV_HALO, CONV_WIDTH),
            lambda i: (jnp.maximum(jnp.minimum(i, n_tiles - 1) * (tm // CONV_HALO) - 1, 0), 0))
        xf = pl.pallas_call(
            functools.partial(_mix_mlp_kernel, tiles_per_seq=seq // tm, n_tiles=n_tiles),
            grid=(n_tiles + 1,),
            in_specs=[tile_spec(d_model), tile_spec(CONV_WIDTH), halo_spec, tile_spec(ATTN_WIDTH),
                      tile_spec(2 * d_model),
                      _const_spec((n_lane_tiles, CONV_KERNEL, LANES)), _const_spec((n_lane_tiles, 1, LANES)),
                      _const_spec((1, CONV_WIDTH)), _weight_spec((CONV_WIDTH, d_model)),
                      _weight_spec((ATTN_WIDTH, d_model)), _weight_spec((d_model, d_model)),
                      _const_spec((1, d_model)), _weight_spec((n_chunks, d_model, FF_CHUNK)),
                      _weight_spec((D_FF, d_model))],
            out_specs=pl.BlockSpec((tm, d_model), lambda i: (jnp.maximum(i - 1, 0), 0)),
            out_shape=jax.ShapeDtypeStruct((n_tok, d_model), _F32),
            scratch_shapes=[pltpu.VMEM((CONV_WIDTH // LANES, CONV_HALO + tm, LANES), _F32),
                            pltpu.VMEM((CONV_WIDTH // LANES, tm, LANES), _F32),
                            pltpu.VMEM((2, tm, d_model), _F32),
                            pltpu.VMEM((tm, d_model), _BF16)],
            compiler_params=_params(1),
            name="mixer_mlp",
        )(xf, u, u, ow, gates,
          w_dw[layer].reshape(CONV_KERNEL, n_lane_tiles, LANES).transpose(1, 0, 2),
          b_dw[layer].reshape(n_lane_tiles, 1, LANES),
          w_conv_norm[layer].reshape(1, CONV_WIDTH), w_conv_out[layer].astype(_BF16),
          w_attn_out[layer].astype(_BF16), w_o[layer].astype(_BF16),
          w_norm_mlp[layer].reshape(1, d_model),
          w_mlp_in[layer].astype(_BF16).reshape(d_model, n_chunks, FF_CHUNK).transpose(1, 0, 2),
          w_mlp_out[layer].astype(_BF16))

    return xf.reshape(batch, seq, d_model)
```

```python
import functools

import jax
import jax.numpy as jnp
from jax import lax
from jax.experimental import pallas as pl
from jax.experimental.pallas import tpu as pltpu

D_MODEL = 1024
HEAD_DIM = 64
HEADS_PER_GROUP = 4
GROUP_WIDTH = HEADS_PER_GROUP * HEAD_DIM
DILATIONS = (1, 4, 16)
SPAN = 128
N_GROUPS = len(DILATIONS)
ATTN_WIDTH = N_GROUPS * GROUP_WIDTH
ROPE_THETA = 500000.0
ROPE_DIM = HEAD_DIM // 4
CONV_WIDTH = D_MODEL
CONV_KERNEL = 31
D_FF = 4 * D_MODEL
NORM_EPS = 1e-6
MASK_VALUE = -1e30

LANES = 128
CONV_HALO = 32
ATTN_TILE = SPAN * DILATIONS[-1]
TOKEN_TILE = 512
CONV_ROWS = 32
CONV_STRIDE = 4
FF_CHUNK = 2048
COMBINE_ROWS = 256
VMEM_LIMIT = 56 * 1024 * 1024

_F32 = jnp.float32
_BF16 = jnp.bfloat16


def _dot(a, b):
    return jnp.dot(a, b, preferred_element_type=_F32)


def _sigmoid(z):
    return 0.5 * jnp.tanh(0.5 * z) + 0.5


def _rms_scale(xf):
    return lax.rsqrt(jnp.mean(xf * xf, axis=-1, keepdims=True) + NORM_EPS)


def _inproj_kernel(x_ref, pos_ref, wn_ref, w_ref, bg_ref, qkw_ref, freq_ref, sel_ref, aux_ref, seg_ref,
                   u_ref, q1_ref, q2_ref, q3_ref, k1_ref, k2_ref, k3_ref, v1_ref, v2_ref, v3_ref, g_ref,
                   z_ref):
    tm = x_ref.shape[0]
    x = x_ref[...]
    n = (x * _rms_scale(x) * wn_ref[...]).astype(_BF16)

    ang = freq_ref[...] * pos_ref[0]
    cs = jnp.concatenate([jnp.cos(ang), jnp.sin(ang)], axis=0).T
    cs_hi = cs.astype(_BF16)
    cs_lo = (cs - cs_hi.astype(_F32)).astype(_BF16)
    tables = _dot(cs_hi, sel_ref[...]) + _dot(cs_lo, sel_ref[...])
    cos = tables[:, 0:LANES] + aux_ref[0:1, :]
    sin = tables[:, LANES:2 * LANES]
    first_half = aux_ref[1:2, :] > 0.5
    seg = seg_ref[...]

    def store_by_residue(slot, out_ref, dil):
        for t in range(GROUP_WIDTH // LANES):
            for r in range(dil):
                rows = z_ref[slot, t] if dil == 1 else z_ref[slot, t, pl.ds(r, tm // dil, stride=dil), :]
                c = r * GROUP_WIDTH + t * LANES
                out_ref[:, c:c + LANES] = rows.astype(_BF16)

    def finish_qk(z, slot, gain_row, scale, out_ref, dil):
        ss = _dot((z * z).astype(_BF16), seg)
        z = z * lax.rsqrt(ss * (1.0 / HEAD_DIM) + NORM_EPS) * qkw_ref[gain_row:gain_row + 1, :]
        for t in range(GROUP_WIDTH // LANES):
            zt = z[:, t * LANES:(t + 1) * LANES]
            partner = jnp.where(first_half, pltpu.roll(zt, LANES - ROPE_DIM // 2, 1),
                                pltpu.roll(zt, ROPE_DIM // 2, 1))
            z_ref[slot, t] = (zt * cos + partner * sin) * scale
        store_by_residue(slot, out_ref, dil)

    def finish_v(z, slot, out_ref, dil):
        for t in range(GROUP_WIDTH // LANES):
            z_ref[slot, t] = z[:, t * LANES:(t + 1) * LANES]
        store_by_residue(slot, out_ref, dil)

    def finish_gate(z, t):
        z = z + bg_ref[:, t * D_MODEL:(t + 1) * D_MODEL]
        g_ref[:, t * D_MODEL:(t + 1) * D_MODEL] = _sigmoid(z).astype(_BF16)

    def finish_glu(z):
        u_ref[...] = (z[:, 0:CONV_WIDTH] * _sigmoid(z[:, CONV_WIDTH:2 * CONV_WIDTH])).astype(_BF16)

    c0 = 2 * CONV_WIDTH
    c3 = c0 + 3 * ATTN_WIDTH
    q_refs = (q1_ref, q2_ref, q3_ref)
    k_refs = (k1_ref, k2_ref, k3_ref)
    v_refs = (v1_ref, v2_ref, v3_ref)

    segments = [(c3, D_MODEL, functools.partial(finish_gate, t=0)),
                (c3 + D_MODEL, D_MODEL, functools.partial(finish_gate, t=1))]
    for g, dil in enumerate(DILATIONS):
        segments.append((c0 + g * GROUP_WIDTH, GROUP_WIDTH,
                         functools.partial(finish_qk, slot=0, gain_row=0, scale=HEAD_DIM ** -0.5,
                                           out_ref=q_refs[g], dil=dil)))
        segments.append((c0 + ATTN_WIDTH + g * GROUP_WIDTH, GROUP_WIDTH,
                         functools.partial(finish_qk, slot=1, gain_row=1, scale=1.0,
                                           out_ref=k_refs[g], dil=dil)))
    segments.append((0, 2 * CONV_WIDTH, finish_glu))
    for g, dil in enumerate(DILATIONS):
        segments.append((c0 + 2 * ATTN_WIDTH + g * GROUP_WIDTH, GROUP_WIDTH,
                         functools.partial(finish_v, slot=g % 2, out_ref=v_refs[g], dil=dil)))

    pending = None
    for col, width, finish in segments:
        z = _dot(n, w_ref[:, col:col + width])
        if pending is not None:
            pending()
        pending = functools.partial(finish, z)
    pending()


def _attn_kernel(q1_ref, k1_ref, k1p_ref, v1_ref, v1p_ref,
                 q2_ref, k2_ref, k2p_ref, v2_ref, v2p_ref,
                 q3_ref, k3_ref, k3p_ref, v3_ref, v3p_ref,
                 ow_ref,
                 kx1, vx1, kx2, vx2, kx3, vx3, o_scr, l_scr):
    tile = pl.program_id(1)

    for kx, vx, kp, kc, vp, vc in ((kx1, vx1, k1p_ref, k1_ref, v1p_ref, v1_ref),
                                   (kx2, vx2, k2p_ref, k2_ref, v2p_ref, v2_ref),
                                   (kx3, vx3, k3p_ref, k3_ref, v3p_ref, v3_ref)):
        kx[0:SPAN, :] = kp[...]
        kx[SPAN:, :] = kc[...]
        vx[0:SPAN, :] = vp[...]
        vx[SPAN:, :] = vc[...]

    qi = lax.broadcasted_iota(jnp.int32, (SPAN, 2 * SPAN), 0)
    ki = lax.broadcasted_iota(jnp.int32, (SPAN, 2 * SPAN), 1)
    dist = qi + SPAN - ki
    band = jnp.where((dist >= 0) & (dist <= SPAN), 0.0, MASK_VALUE).astype(_F32)
    band_first = jnp.where((ki >= SPAN) | (tile > 0), band, MASK_VALUE)
    band2 = jnp.concatenate([band, band], axis=0)
    band_first2 = jnp.concatenate([band_first, band_first], axis=0)

    lane = lax.broadcasted_iota(jnp.int32, (SPAN, LANES), 1)
    low_head = lane < HEAD_DIM

    def unit(q_ref, kx, vx, row0, col0, bias2, tok_rows, out_tile0):
        for pair in range(GROUP_WIDTH // LANES):
            c = col0 + pair * LANES
            q = q_ref[pl.ds(row0, SPAN), c:c + LANES]
            kk = kx[pl.ds(row0, 2 * SPAN), c:c + LANES]
            vv = vx[pl.ds(row0, 2 * SPAN), c:c + LANES]
            zero = jnp.zeros_like(q)
            q2 = jnp.concatenate([jnp.where(low_head, q, zero), jnp.where(low_head, zero, q)], axis=0)
            s = lax.dot_general(q2, kk, (((1,), (1,)), ((), ())), preferred_element_type=_F32) + bias2
            m = jnp.max(s, axis=-1, keepdims=True)
            e = jnp.exp(s - m)
            l = jnp.sum(e, axis=-1, keepdims=True)
            o2 = _dot(e.astype(_BF16), vv) * (1.0 / l)
            lse2 = jnp.broadcast_to(m + jnp.log(l), (2 * SPAN, LANES))
            ot = out_tile0 + pair
            o_scr[ot, tok_rows, :] = jnp.where(low_head, o2[0:SPAN], o2[SPAN:2 * SPAN])
            l_scr[ot, tok_rows, :] = jnp.where(low_head, lse2[0:SPAN], lse2[SPAN:2 * SPAN])

    for j in range(ATTN_TILE // SPAN):
        unit(q1_ref, kx1, vx1, j * SPAN, 0, band_first2 if j == 0 else band2, pl.ds(j * SPAN, SPAN), 0)

    for g in (1, 2):
        dil = DILATIONS[g]
        q_ref, kx, vx = ((q2_ref, kx2, vx2), (q3_ref, kx3, vx3))[g - 1]
        for r in range(dil):
            for j in range(ATTN_TILE // dil // SPAN):
                unit(q_ref, kx, vx, j * SPAN, r * GROUP_WIDTH, band_first2 if j == 0 else band2,
                     pl.ds(dil * SPAN * j + r, SPAN, stride=dil), g * (GROUP_WIDTH // LANES))

    def combine(i, carry):
        rows = pl.ds(pl.multiple_of(i * COMBINE_ROWS, COMBINE_ROWS), COMBINE_ROWS)
        tiles_per_group = GROUP_WIDTH // LANES
        for t in range(tiles_per_group):
            ls = [l_scr[g * tiles_per_group + t, rows, :] for g in range(N_GROUPS)]
            mx = jnp.maximum(jnp.maximum(ls[0], ls[1]), ls[2])
            es = [jnp.exp(l - mx) for l in ls]
            inv = 1.0 / (es[0] + es[1] + es[2])
            for g in range(N_GROUPS):
                ot = g * tiles_per_group + t
                ow_ref[rows, ot * LANES:(ot + 1) * LANES] = (o_scr[ot, rows, :] * (es[g] * inv)).astype(_BF16)
        return carry

    lax.fori_loop(0, ATTN_TILE // COMBINE_ROWS, combine, 0)


def _mix_mlp_kernel(x_ref, u_ref, uh_ref, ow_ref, g_ref,
                    wdw_ref, bdw_ref, wcn_ref, wco_ref, wao_ref, wo_ref, wnm_ref, w1_ref, w2_ref,
                    out_ref, ext_ref, y_ref, x1_ref, hn_ref, *, tiles_per_seq, n_tiles):
    step = pl.program_id(0)
    tile = jnp.minimum(step, n_tiles - 1)
    first = (tile % tiles_per_seq) == 0
    tm = x_ref.shape[0]
    cur = step % 2
    prev = 1 - cur

    @pl.when(step == 0)
    def _():
        x1_ref[1] = jnp.zeros((tm, D_MODEL), _F32)

    shift = CONV_HALO - (CONV_KERNEL - 1)
    block = CONV_ROWS * CONV_STRIDE
    n_lane_tiles = CONV_WIDTH // LANES
    n_chunks = D_FF // FF_CHUNK
    lane_tiles_per_chunk = n_lane_tiles // n_chunks
    assert lane_tiles_per_chunk * n_chunks == n_lane_tiles

    for c in range(n_lane_tiles):
        cols = slice(c * LANES, (c + 1) * LANES)
        halo = uh_ref[:, cols].astype(_F32)
        ext_ref[c, 0:CONV_HALO, :] = jnp.where(first, jnp.zeros_like(halo), halo)
        ext_ref[c, CONV_HALO:, :] = u_ref[:, cols].astype(_F32)

    def conv_lane_tile(c):
        taps = [jnp.broadcast_to(wdw_ref[c, k:k + 1, :], (CONV_ROWS, LANES)) for k in range(CONV_KERNEL)]
        bias = jnp.broadcast_to(bdw_ref[c], (CONV_ROWS, LANES))

        for base in range(0, tm, block):
            accs = [bias] * CONV_STRIDE
            for s in range(CONV_KERNEL + CONV_STRIDE - 1):
                rows = ext_ref[c, pl.ds(base + shift + s, CONV_ROWS, stride=CONV_STRIDE), :]
                for phase in range(CONV_STRIDE):
                    if 0 <= s - phase < CONV_KERNEL:
                        accs[phase] = accs[phase] + rows * taps[s - phase]
            for phase in range(CONV_STRIDE):
                y_ref[c, pl.ds(base + phase, CONV_ROWS, stride=CONV_STRIDE), :] = accs[phase]

    xp = x1_ref[prev]
    hn_ref[...] = (xp * _rms_scale(xp) * wnm_ref[...]).astype(_BF16)
    out_ref[...] = xp

    def chunk_body(c, carry):
        h = jnp.maximum(_dot(hn_ref[...], w1_ref[c]), 0.0)
        rows = pl.ds(pl.multiple_of(c * FF_CHUNK, FF_CHUNK), FF_CHUNK)
        out_ref[...] += _dot((h * h).astype(_BF16), w2_ref[rows, :])
        for t in range(lane_tiles_per_chunk):
            conv_lane_tile(c * lane_tiles_per_chunk + t)
        return carry

    lax.fori_loop(0, n_chunks, chunk_body, 0)

    y = jnp.concatenate([y_ref[c] for c in range(n_lane_tiles)], axis=1)
    y = y * _rms_scale(y) * wcn_ref[...]
    y = y * _sigmoid(y)
    branch_a = _dot(y.astype(_BF16), wco_ref[...])
    branch_b = _dot(ow_ref[...], wao_ref[...])
    merged = (g_ref[:, 0:D_MODEL].astype(_F32) * branch_a
              + g_ref[:, D_MODEL:2 * D_MODEL].astype(_F32) * branch_b)
    x1_ref[cur] = x_ref[...] + _dot(merged.astype(_BF16), wo_ref[...])


def _const_spec(shape):
    return pl.BlockSpec(shape, lambda *_: (0,) * len(shape))


def _weight_spec(shape):
    return pl.BlockSpec(shape, lambda *_: (0,) * len(shape), pipeline_mode=pl.Buffered(1))


def _rope_constants(tm):
    half = ROPE_DIM // 2
    inv_freq = ROPE_THETA ** (-jnp.arange(0, ROPE_DIM, 2, dtype=_F32) / ROPE_DIM)
    freq = jnp.broadcast_to(inv_freq[:, None], (half, tm))
    d = jnp.arange(LANES) % HEAD_DIM
    rotary = d < ROPE_DIM
    hit = (rotary[None, :] & ((d % half)[None, :] == jnp.arange(half)[:, None])).astype(_F32)
    sign = jnp.where(d < half, -1.0, 1.0)[None, :]
    zero = jnp.zeros_like(hit)
    sel = jnp.concatenate([jnp.concatenate([hit, zero], axis=1),
                           jnp.concatenate([zero, hit * sign], axis=1)], axis=0).astype(_BF16)
    aux = jnp.stack([1.0 - rotary.astype(_F32), (d < half).astype(_F32)]
                    + [jnp.zeros((LANES,), _F32)] * 6)
    return freq, sel, aux


def _params(n_axes):
    return pltpu.CompilerParams(dimension_semantics=("arbitrary",) * n_axes, vmem_limit_bytes=VMEM_LIMIT)


def kernel(x, positions, w_norm_mix, w_in, b_gate, w_dw, b_dw, w_conv_norm, w_conv_out, q_norm_w, k_norm_w,
           w_attn_out, w_o, w_norm_mlp, w_mlp_in, w_mlp_out):
    batch, seq, d_model = x.shape
    depth = w_in.shape[0]
    assert d_model == D_MODEL and seq % ATTN_TILE == 0 and ATTN_TILE % TOKEN_TILE == 0
    n_tok = batch * seq
    tm = TOKEN_TILE
    n_tiles = n_tok // tm
    in_width = w_in.shape[-1]

    xf = x.reshape(n_tok, d_model)
    pos = positions.astype(_F32).reshape(n_tiles, 1, tm)
    freq, sel, aux = _rope_constants(tm)
    head = jnp.arange(GROUP_WIDTH) // HEAD_DIM
    seg = (head[:, None] == head[None, :]).astype(_BF16)

    def row_spec(width):
        return pl.BlockSpec((tm, width), lambda i: (i, 0))

    for layer in range(depth):
        qk_gain = jnp.stack([jnp.tile(q_norm_w[layer], GROUP_WIDTH // HEAD_DIM),
                             jnp.tile(k_norm_w[layer], GROUP_WIDTH // HEAD_DIM)]).astype(_F32)

        def grp_spec(dil):
            return pl.BlockSpec((tm // dil, dil * GROUP_WIDTH), lambda i: (i, 0))

        grp_specs = [grp_spec(dil) for dil in DILATIONS] * 3
        grp_shapes = [jax.ShapeDtypeStruct((n_tok // dil, dil * GROUP_WIDTH), _BF16) for dil in DILATIONS] * 3
        outs = pl.pallas_call(
            _inproj_kernel,
            grid=(n_tiles,),
            in_specs=[row_spec(d_model), pl.BlockSpec((1, 1, tm), lambda i: (i, 0, 0)),
                      _const_spec((1, d_model)),
                      _const_spec((d_model, in_width)), _const_spec((1, 2 * d_model)),
                      _const_spec((2, GROUP_WIDTH)), _const_spec((ROPE_DIM // 2, tm)),
                      _const_spec((ROPE_DIM, 2 * LANES)), _const_spec((8, LANES)),
                      _const_spec((GROUP_WIDTH, GROUP_WIDTH))],
            out_specs=[row_spec(CONV_WIDTH)] + grp_specs + [row_spec(2 * d_model)],
            out_shape=[jax.ShapeDtypeStruct((n_tok, CONV_WIDTH), _BF16)] + grp_shapes
                      + [jax.ShapeDtypeStruct((n_tok, 2 * d_model), _BF16)],
            scratch_shapes=[pltpu.VMEM((2, GROUP_WIDTH // LANES, tm, LANES), _F32)],
            compiler_params=_params(1),
            name="inproj",
        )(xf, pos, w_norm_mix[layer].reshape(1, d_model), w_in[layer].astype(_BF16),
          b_gate[layer].reshape(1, 2 * d_model), qk_gain, freq, sel, aux, seg)
        u, qkv, gates = outs[0], outs[1:10], outs[10]

        n_attn_tiles = seq // ATTN_TILE
        in_specs, operands, scratch = [], [], []
        for g, dil in enumerate(DILATIONS):
            rows, width = ATTN_TILE // dil, dil * GROUP_WIDTH
            spans_per_tile, spans_per_seq = rows // SPAN, seq // dil // SPAN
            cur = pl.BlockSpec((rows, width), lambda b, i: (b * n_attn_tiles + i, 0))
            prev = pl.BlockSpec(
                (SPAN, width),
                lambda b, i, n=spans_per_tile, s=spans_per_seq: (b * s + jnp.maximum(i * n - 1, 0), 0))
            in_specs += [cur, cur, prev, cur, prev]
            operands += [qkv[g], qkv[3 + g], qkv[3 + g], qkv[6 + g], qkv[6 + g]]
            scratch += [pltpu.VMEM((SPAN + rows, width), _BF16)] * 2
        scratch += [pltpu.VMEM((ATTN_WIDTH // LANES, ATTN_TILE, LANES), _F32)] * 2
        ow = pl.pallas_call(
            _attn_kernel,
            grid=(batch, n_attn_tiles),
            in_specs=in_specs,
            out_specs=pl.BlockSpec((ATTN_TILE, ATTN_WIDTH), lambda b, i: (b * n_attn_tiles + i, 0)),
            out_shape=jax.ShapeDtypeStruct((n_tok, ATTN_WIDTH), _BF16),
            scratch_shapes=scratch,
            compiler_params=_params(2),
            name="attention",
        )(*operands)

        n_lane_tiles, n_chunks = CONV_WIDTH // LANES, D_FF // FF_CHUNK

        def tile_spec(width):
            return pl.BlockSpec((tm, width), lambda i: (jnp.minimum(i, n_tiles - 1), 0))

        halo_spec = pl.BlockSpec(
            (CONV_HALO, CONV_WIDTH),
            lambda i: (jnp.maximum(jnp.minimum(i, n_tiles - 1) * (tm // CONV_HALO) - 1, 0), 0))
        xf = pl.pallas_call(
            functools.partial(_mix_mlp_kernel, tiles_per_seq=seq // tm, n_tiles=n_tiles),
            grid=(n_tiles + 1,),
            in_specs=[tile_spec(d_model), tile_spec(CONV_WIDTH), halo_spec, tile_spec(ATTN_WIDTH),
                      tile_spec(2 * d_model),
                      _const_spec((n_lane_tiles, CONV_KERNEL, LANES)), _const_spec((n_lane_tiles, 1, LANES)),
                      _const_spec((1, CONV_WIDTH)), _weight_spec((CONV_WIDTH, d_model)),
                      _weight_spec((ATTN_WIDTH, d_model)), _weight_spec((d_model, d_model)),
                      _const_spec((1, d_model)), _weight_spec((n_chunks, d_model, FF_CHUNK)),
                      _weight_spec((D_FF, d_model))],
            out_specs=pl.BlockSpec((tm, d_model), lambda i: (jnp.maximum(i - 1, 0), 0)),
            out_shape=jax.ShapeDtypeStruct((n_tok, d_model), _F32),
            scratch_shapes=[pltpu.VMEM((CONV_WIDTH // LANES, CONV_HALO + tm, LANES), _F32),
                            pltpu.VMEM((CONV_WIDTH // LANES, tm, LANES), _F32),
                            pltpu.VMEM((2, tm, d_model), _F32),
                            pltpu.VMEM((tm, d_model), _BF16)],
            compiler_params=_params(1),
            name="mixer_mlp",
        )(xf, u, u, ow, gates,
          w_dw[layer].reshape(CONV_KERNEL, n_lane_tiles, LANES).transpose(1, 0, 2),
          b_dw[layer].reshape(n_lane_tiles, 1, LANES),
          w_conv_norm[layer].reshape(1, CONV_WIDTH), w_conv_out[layer].astype(_BF16),
          w_attn_out[layer].astype(_BF16), w_o[layer].astype(_BF16),
          w_norm_mlp[layer].reshape(1, d_model),
          w_mlp_in[layer].astype(_BF16).reshape(d_model, n_chunks, FF_CHUNK).transpose(1, 0, 2),
          w_mlp_out[layer].astype(_BF16))

    return xf.reshape(batch, seq, d_model)
```

```python
import functools

import jax
import jax.numpy as jnp
from jax import lax
from jax.experimental import pallas as pl
from jax.experimental.pallas import tpu as pltpu

D_MODEL = 1024
HEAD_DIM = 64
HEADS_PER_GROUP = 4
GROUP_WIDTH = HEADS_PER_GROUP * HEAD_DIM
DILATIONS = (1, 4, 16)
SPAN = 128
N_GROUPS = len(DILATIONS)
ATTN_WIDTH = N_GROUPS * GROUP_WIDTH
ROPE_THETA = 500000.0
ROPE_DIM = HEAD_DIM // 4
CONV_WIDTH = D_MODEL
CONV_KERNEL = 31
D_FF = 4 * D_MODEL
NORM_EPS = 1e-6
MASK_VALUE = -1e30

LANES = 128
CONV_HALO = 32
ATTN_TILE = SPAN * DILATIONS[-1]
TOKEN_TILE = 512
CONV_ROWS = 32
CONV_STRIDE = 4
FF_CHUNK = 2048
COMBINE_ROWS = 256
VMEM_LIMIT = 56 * 1024 * 1024

_F32 = jnp.float32
_BF16 = jnp.bfloat16


def _dot(a, b):
    return jnp.dot(a, b, preferred_element_type=_F32)


def _sigmoid(z):
    return 0.5 * jnp.tanh(0.5 * z) + 0.5


def _rms_scale(xf):
    return lax.rsqrt(jnp.mean(xf * xf, axis=-1, keepdims=True) + NORM_EPS)


def _inproj_kernel(x_ref, pos_ref, wn_ref, w_ref, bg_ref, qkw_ref, freq_ref, sel_ref, aux_ref, seg_ref,
                   u_ref, q1_ref, q2_ref, q3_ref, k1_ref, k2_ref, k3_ref, v1_ref, v2_ref, v3_ref, g_ref,
                   z_ref):
    tm = x_ref.shape[0]
    x = x_ref[...]
    n = (x * _rms_scale(x) * wn_ref[...]).astype(_BF16)

    ang = freq_ref[...] * pos_ref[0]
    cs = jnp.concatenate([jnp.cos(ang), jnp.sin(ang)], axis=0).T
    cs_hi = cs.astype(_BF16)
    cs_lo = (cs - cs_hi.astype(_F32)).astype(_BF16)
    tables = _dot(cs_hi, sel_ref[...]) + _dot(cs_lo, sel_ref[...])
    cos = tables[:, 0:LANES] + aux_ref[0:1, :]
    sin = tables[:, LANES:2 * LANES]
    first_half = aux_ref[1:2, :] > 0.5
    seg = seg_ref[...]

    def store_by_residue(slot, out_ref, dil):
        for t in range(GROUP_WIDTH // LANES):
            for r in range(dil):
                rows = z_ref[slot, t] if dil == 1 else z_ref[slot, t, pl.ds(r, tm // dil, stride=dil), :]
                c = r * GROUP_WIDTH + t * LANES
                out_ref[:, c:c + LANES] = rows.astype(_BF16)

    def finish_qk(z, slot, gain_row, scale, out_ref, dil):
        ss = _dot((z * z).astype(_BF16), seg)
        z = z * lax.rsqrt(ss * (1.0 / HEAD_DIM) + NORM_EPS) * qkw_ref[gain_row:gain_row + 1, :]
        for t in range(GROUP_WIDTH // LANES):
            zt = z[:, t * LANES:(t + 1) * LANES]
            partner = jnp.where(first_half, pltpu.roll(zt, LANES - ROPE_DIM // 2, 1),
                                pltpu.roll(zt, ROPE_DIM // 2, 1))
            z_ref[slot, t] = (zt * cos + partner * sin) * scale
        store_by_residue(slot, out_ref, dil)

    def finish_v(z, slot, out_ref, dil):
        for t in range(GROUP_WIDTH // LANES):
            z_ref[slot, t] = z[:, t * LANES:(t + 1) * LANES]
        store_by_residue(slot, out_ref, dil)

    def finish_gate(z, t):
        z = z + bg_ref[:, t * D_MODEL:(t + 1) * D_MODEL]
        g_ref[:, t * D_MODEL:(t + 1) * D_MODEL] = _sigmoid(z).astype(_BF16)

    def finish_glu(z):
        u_ref[...] = (z[:, 0:CONV_WIDTH] * _sigmoid(z[:, CONV_WIDTH:2 * CONV_WIDTH])).astype(_BF16)

    c0 = 2 * CONV_WIDTH
    c3 = c0 + 3 * ATTN_WIDTH
    q_refs = (q1_ref, q2_ref, q3_ref)
    k_refs = (k1_ref, k2_ref, k3_ref)
    v_refs = (v1_ref, v2_ref, v3_ref)

    segments = [(c3, D_MODEL, functools.partial(finish_gate, t=0)),
                (c3 + D_MODEL, D_MODEL, functools.partial(finish_gate, t=1))]
    for g, dil in enumerate(DILATIONS):
        segments.append((c0 + g * GROUP_WIDTH, GROUP_WIDTH,
                         functools.partial(finish_qk, slot=0, gain_row=0, scale=HEAD_DIM ** -0.5,
                                           out_ref=q_refs[g], dil=dil)))
        segments.append((c0 + ATTN_WIDTH + g * GROUP_WIDTH, GROUP_WIDTH,
                         functools.partial(finish_qk, slot=1, gain_row=1, scale=1.0,
                                           out_ref=k_refs[g], dil=dil)))
    segments.append((0, 2 * CONV_WIDTH, finish_glu))
    for g, dil in enumerate(DILATIONS):
        segments.append((c0 + 2 * ATTN_WIDTH + g * GROUP_WIDTH, GROUP_WIDTH,
                         functools.partial(finish_v, slot=g % 2, out_ref=v_refs[g], dil=dil)))

    pending = None
    for col, width, finish in segments:
        z = _dot(n, w_ref[:, col:col + width])
        if pending is not None:
            pending()
        pending = functools.partial(finish, z)
    pending()


def _attn_kernel(q1_ref, k1_ref, k1p_ref, v1_ref, v1p_ref,
                 q2_ref, k2_ref, k2p_ref, v2_ref, v2p_ref,
                 q3_ref, k3_ref, k3p_ref, v3_ref, v3p_ref,
                 ow_ref,
                 o_scr, l_scr, m_scr):
    tile = pl.program_id(1)

    qi = lax.broadcasted_iota(jnp.int32, (SPAN, 2 * SPAN), 0)
    ki = lax.broadcasted_iota(jnp.int32, (SPAN, 2 * SPAN), 1)
    dist = qi + SPAN - ki
    band = jnp.where((dist >= 0) & (dist <= SPAN), 0.0, MASK_VALUE).astype(_F32)
    band_first = jnp.where((ki >= SPAN) | (tile > 0), band, MASK_VALUE)
    band2 = jnp.concatenate([band, band], axis=0)
    band_first2 = jnp.concatenate([band_first, band_first], axis=0)

    lane = lax.broadcasted_iota(jnp.int32, (SPAN, LANES), 1)
    low_head = lane < HEAD_DIM
    ones = jnp.ones((2 * SPAN, LANES), _BF16)

    def with_prev(cur_ref, prev_ref, j, cols):
        if j == 0:
            return jnp.concatenate([prev_ref[:, cols], cur_ref[0:SPAN, cols]], axis=0)
        return cur_ref[(j - 1) * SPAN:(j + 1) * SPAN, cols]

    def unit(refs, j, col0, tok_rows, out_tile0):
        q_ref, k_ref, kp_ref, v_ref, vp_ref = refs
        bias2 = band_first2 if j == 0 else band2
        for pair in range(GROUP_WIDTH // LANES):
            cols = slice(col0 + pair * LANES, col0 + (pair + 1) * LANES)
            q = q_ref[j * SPAN:(j + 1) * SPAN, cols]
            kk = with_prev(k_ref, kp_ref, j, cols)
            vv = with_prev(v_ref, vp_ref, j, cols)
            zero = jnp.zeros_like(q)
            q2 = jnp.concatenate([jnp.where(low_head, q, zero), jnp.where(low_head, zero, q)], axis=0)
            s = lax.dot_general(q2, kk, (((1,), (1,)), ((), ())), preferred_element_type=_F32) + bias2
            m = jnp.max(s, axis=-1, keepdims=True)
            p = jnp.exp(s - m).astype(_BF16)
            o2 = _dot(p, jnp.concatenate([vv, ones], axis=1))
            m2 = jnp.broadcast_to(m, (2 * SPAN, LANES))
            ot = out_tile0 + pair
            o_scr[ot, tok_rows, :] = jnp.where(low_head, o2[0:SPAN, 0:LANES], o2[SPAN:2 * SPAN, 0:LANES])
            l_scr[ot, tok_rows, :] = jnp.where(low_head, o2[0:SPAN, LANES:2 * LANES],
                                               o2[SPAN:2 * SPAN, LANES:2 * LANES])
            m_scr[ot, tok_rows, :] = jnp.where(low_head, m2[0:SPAN], m2[SPAN:2 * SPAN])

    group_refs = ((q1_ref, k1_ref, k1p_ref, v1_ref, v1p_ref),
                  (q2_ref, k2_ref, k2p_ref, v2_ref, v2p_ref),
                  (q3_ref, k3_ref, k3p_ref, v3_ref, v3p_ref))
    for g, dil in enumerate(DILATIONS):
        for r in range(dil):
            for j in range(ATTN_TILE // dil // SPAN):
                tok_rows = pl.ds(dil * SPAN * j + r, SPAN) if dil == 1 else pl.ds(dil * SPAN * j + r, SPAN, stride=dil)
                unit(group_refs[g], j, r * GROUP_WIDTH, tok_rows, g * (GROUP_WIDTH // LANES))

    def combine(i, carry):
        rows = pl.ds(pl.multiple_of(i * COMBINE_ROWS, COMBINE_ROWS), COMBINE_ROWS)
        tiles_per_group = GROUP_WIDTH // LANES
        for t in range(tiles_per_group):
            tiles = [g * tiles_per_group + t for g in range(N_GROUPS)]
            ms = [m_scr[ot, rows, :] for ot in tiles]
            mx = jnp.maximum(jnp.maximum(ms[0], ms[1]), ms[2])
            ws = [jnp.exp(m - mx) for m in ms]
            den = sum(l_scr[ot, rows, :] * w for ot, w in zip(tiles, ws))
            inv = 1.0 / den
            for ot, w in zip(tiles, ws):
                ow_ref[rows, ot * LANES:(ot + 1) * LANES] = (o_scr[ot, rows, :] * (w * inv)).astype(_BF16)
        return carry

    lax.fori_loop(0, ATTN_TILE // COMBINE_ROWS, combine, 0)


def _mix_mlp_kernel(x_ref, u_ref, uh_ref, ow_ref, g_ref,
                    wdw_ref, bdw_ref, wcn_ref, wco_ref, wao_ref, wo_ref, wnm_ref, w1_ref, w2_ref,
                    out_ref, ext_ref, y_ref, x1_ref, hn_ref, *, tiles_per_seq, n_tiles):
    step = pl.program_id(0)
    tile = jnp.minimum(step, n_tiles - 1)
    first = (tile % tiles_per_seq) == 0
    tm = x_ref.shape[0]
    cur = step % 2
    prev = 1 - cur

    @pl.when(step == 0)
    def _():
        x1_ref[1] = jnp.zeros((tm, D_MODEL), _F32)

    shift = CONV_HALO - (CONV_KERNEL - 1)
    block = CONV_ROWS * CONV_STRIDE
    n_lane_tiles = CONV_WIDTH // LANES
    n_chunks = D_FF // FF_CHUNK
    lane_tiles_per_chunk = n_lane_tiles // n_chunks
    assert lane_tiles_per_chunk * n_chunks == n_lane_tiles

    for c in range(n_lane_tiles):
        cols = slice(c * LANES, (c + 1) * LANES)
        halo = uh_ref[:, cols].astype(_F32)
        ext_ref[c, 0:CONV_HALO, :] = jnp.where(first, jnp.zeros_like(halo), halo)
        ext_ref[c, CONV_HALO:, :] = u_ref[:, cols].astype(_F32)

    def conv_lane_tile(c):
        taps = [jnp.broadcast_to(wdw_ref[c, k:k + 1, :], (CONV_ROWS, LANES)) for k in range(CONV_KERNEL)]
        bias = jnp.broadcast_to(bdw_ref[c], (CONV_ROWS, LANES))

        for base in range(0, tm, block):
            accs = [bias] * CONV_STRIDE
            for s in range(CONV_KERNEL + CONV_STRIDE - 1):
                rows = ext_ref[c, pl.ds(base + shift + s, CONV_ROWS, stride=CONV_STRIDE), :]
                for phase in range(CONV_STRIDE):
                    if 0 <= s - phase < CONV_KERNEL:
                        accs[phase] = accs[phase] + rows * taps[s - phase]
            for phase in range(CONV_STRIDE):
                y_ref[c, pl.ds(base + phase, CONV_ROWS, stride=CONV_STRIDE), :] = accs[phase]

    xp = x1_ref[prev]
    hn_ref[...] = (xp * _rms_scale(xp) * wnm_ref[...]).astype(_BF16)
    out_ref[...] = xp

    def chunk_body(c, carry):
        h = jnp.maximum(_dot(hn_ref[...], w1_ref[c]), 0.0)
        rows = pl.ds(pl.multiple_of(c * FF_CHUNK, FF_CHUNK), FF_CHUNK)
        out_ref[...] += _dot((h * h).astype(_BF16), w2_ref[rows, :])
        for t in range(lane_tiles_per_chunk):
            conv_lane_tile(c * lane_tiles_per_chunk + t)
        return carry

    lax.fori_loop(0, n_chunks, chunk_body, 0)

    y = jnp.concatenate([y_ref[c] for c in range(n_lane_tiles)], axis=1)
    y = y * _rms_scale(y) * wcn_ref[...]
    y = y * _sigmoid(y)
    branch_a = _dot(y.astype(_BF16), wco_ref[...])
    branch_b = _dot(ow_ref[...], wao_ref[...])
    merged = (g_ref[:, 0:D_MODEL].astype(_F32) * branch_a
              + g_ref[:, D_MODEL:2 * D_MODEL].astype(_F32) * branch_b)
    x1_ref[cur] = x_ref[...] + _dot(merged.astype(_BF16), wo_ref[...])


def _const_spec(shape):
    return pl.BlockSpec(shape, lambda *_: (0,) * len(shape))


def _weight_spec(shape):
    return pl.BlockSpec(shape, lambda *_: (0,) * len(shape), pipeline_mode=pl.Buffered(1))


def _rope_constants(tm):
    half = ROPE_DIM // 2
    inv_freq = ROPE_THETA ** (-jnp.arange(0, ROPE_DIM, 2, dtype=_F32) / ROPE_DIM)
    freq = jnp.broadcast_to(inv_freq[:, None], (half, tm))
    d = jnp.arange(LANES) % HEAD_DIM
    rotary = d < ROPE_DIM
    hit = (rotary[None, :] & ((d % half)[None, :] == jnp.arange(half)[:, None])).astype(_F32)
    sign = jnp.where(d < half, -1.0, 1.0)[None, :]
    zero = jnp.zeros_like(hit)
    sel = jnp.concatenate([jnp.concatenate([hit, zero], axis=1),
                           jnp.concatenate([zero, hit * sign], axis=1)], axis=0).astype(_BF16)
    aux = jnp.stack([1.0 - rotary.astype(_F32), (d < half).astype(_F32)]
                    + [jnp.zeros((LANES,), _F32)] * 6)
    return freq, sel, aux


def _params(n_axes):
    return pltpu.CompilerParams(dimension_semantics=("arbitrary",) * n_axes, vmem_limit_bytes=VMEM_LIMIT)


def kernel(x, positions, w_norm_mix, w_in, b_gate, w_dw, b_dw, w_conv_norm, w_conv_out, q_norm_w, k_norm_w,
           w_attn_out, w_o, w_norm_mlp, w_mlp_in, w_mlp_out):
    batch, seq, d_model = x.shape
    depth = w_in.shape[0]
    assert d_model == D_MODEL and seq % ATTN_TILE == 0 and ATTN_TILE % TOKEN_TILE == 0
    n_tok = batch * seq
    tm = TOKEN_TILE
    n_tiles = n_tok // tm
    in_width = w_in.shape[-1]

    xf = x.reshape(n_tok, d_model)
    pos = positions.astype(_F32).reshape(n_tiles, 1, tm)
    freq, sel, aux = _rope_constants(tm)
    head = jnp.arange(GROUP_WIDTH) // HEAD_DIM
    seg = (head[:, None] == head[None, :]).astype(_BF16)

    def row_spec(width):
        return pl.BlockSpec((tm, width), lambda i: (i, 0))

    for layer in range(depth):
        qk_gain = jnp.stack([jnp.tile(q_norm_w[layer], GROUP_WIDTH // HEAD_DIM),
                             jnp.tile(k_norm_w[layer], GROUP_WIDTH // HEAD_DIM)]).astype(_F32)

        def grp_spec(dil):
            return pl.BlockSpec((tm // dil, dil * GROUP_WIDTH), lambda i: (i, 0))

        grp_specs = [grp_spec(dil) for dil in DILATIONS] * 3
        grp_shapes = [jax.ShapeDtypeStruct((n_tok // dil, dil * GROUP_WIDTH), _BF16) for dil in DILATIONS] * 3
        outs = pl.pallas_call(
            _inproj_kernel,
            grid=(n_tiles,),
            in_specs=[row_spec(d_model), pl.BlockSpec((1, 1, tm), lambda i: (i, 0, 0)),
                      _const_spec((1, d_model)),
                      _const_spec((d_model, in_width)), _const_spec((1, 2 * d_model)),
                      _const_spec((2, GROUP_WIDTH)), _const_spec((ROPE_DIM // 2, tm)),
                      _const_spec((ROPE_DIM, 2 * LANES)), _const_spec((8, LANES)),
                      _const_spec((GROUP_WIDTH, GROUP_WIDTH))],
            out_specs=[row_spec(CONV_WIDTH)] + grp_specs + [row_spec(2 * d_model)],
            out_shape=[jax.ShapeDtypeStruct((n_tok, CONV_WIDTH), _BF16)] + grp_shapes
                      + [jax.ShapeDtypeStruct((n_tok, 2 * d_model), _BF16)],
            scratch_shapes=[pltpu.VMEM((2, GROUP_WIDTH // LANES, tm, LANES), _F32)],
            compiler_params=_params(1),
            name="inproj",
        )(xf, pos, w_norm_mix[layer].reshape(1, d_model), w_in[layer].astype(_BF16),
          b_gate[layer].reshape(1, 2 * d_model), qk_gain, freq, sel, aux, seg)
        u, qkv, gates = outs[0], outs[1:10], outs[10]

        n_attn_tiles = seq // ATTN_TILE
        in_specs, operands = [], []
        for g, dil in enumerate(DILATIONS):
            rows, width = ATTN_TILE // dil, dil * GROUP_WIDTH
            spans_per_tile, spans_per_seq = rows // SPAN, seq // dil // SPAN
            cur = pl.BlockSpec((rows, width), lambda b, i: (b * n_attn_tiles + i, 0))
            prev = pl.BlockSpec(
                (SPAN, width),
                lambda b, i, n=spans_per_tile, s=spans_per_seq: (b * s + jnp.maximum(i * n - 1, 0), 0))
            in_specs += [cur, cur, prev, cur, prev]
            operands += [qkv[g], qkv[3 + g], qkv[3 + g], qkv[6 + g], qkv[6 + g]]
        scratch = [pltpu.VMEM((ATTN_WIDTH // LANES, ATTN_TILE, LANES), _F32)] * 3
        ow = pl.pallas_call(
            _attn_kernel,
            grid=(batch, n_attn_tiles),
            in_specs=in_specs,
            out_specs=pl.BlockSpec((ATTN_TILE, ATTN_WIDTH), lambda b, i: (b * n_attn_tiles + i, 0)),
            out_shape=jax.ShapeDtypeStruct((n_tok, ATTN_WIDTH), _BF16),
            scratch_shapes=scratch,
            compiler_params=_params(2),
            name="attention",
        )(*operands)

        n_lane_tiles, n_chunks = CONV_WIDTH // LANES, D_FF // FF_CHUNK

        def tile_spec(width):
            return pl.BlockSpec((tm, width), lambda i: (jnp.minimum(i, n_tiles - 1), 0))

        halo_spec = pl.BlockSpec(
            (CONV_HALO, CONV_WIDTH),
            lambda i: (jnp.maximum(jnp.minimum(i, n_tiles - 1) * (tm // CONV_HALO) - 1, 0), 0))
        xf = pl.pallas_call(
            functools.partial(_mix_mlp_kernel, tiles_per_seq=seq // tm, n_tiles=n_tiles),
            grid=(n_tiles + 1,),
            in_specs=[tile_spec(d_model), tile_spec(CONV_WIDTH), halo_spec, tile_spec(ATTN_WIDTH),
                      tile_spec(2 * d_model),
                      _const_spec((n_lane_tiles, CONV_KERNEL, LANES)), _const_spec((n_lane_tiles, 1, LANES)),
                      _const_spec((1, CONV_WIDTH)), _weight_spec((CONV_WIDTH, d_model)),
                      _weight_spec((ATTN_WIDTH, d_model)), _weight_spec((d_model, d_model)),
                      _const_spec((1, d_model)), _weight_spec((n_chunks, d_model, FF_CHUNK)),
                      _weight_spec((D_FF, d_model))],
            out_specs=pl.BlockSpec((tm, d_model), lambda i: (jnp.maximum(i - 1, 0), 0)),
            out_shape=jax.ShapeDtypeStruct((n_tok, d_model), _F32),
            scratch_shapes=[pltpu.VMEM((CONV_WIDTH // LANES, CONV_HALO + tm, LANES), _F32),
                            pltpu.VMEM((CONV_WIDTH // LANES, tm, LANES), _F32),
                            pltpu.VMEM((2, tm, d_model), _F32),
                            pltpu.VMEM((tm, d_model), _BF16)],
            compiler_params=_params(1),
            name="mixer_mlp",
        )(xf, u, u, ow, gates,
          w_dw[layer].reshape(CONV_KERNEL, n_lane_tiles, LANES).transpose(1, 0, 2),
          b_dw[layer].reshape(n_lane_tiles, 1, LANES),
          w_conv_norm[layer].reshape(1, CONV_WIDTH), w_conv_out[layer].astype(_BF16),
          w_attn_out[layer].astype(_BF16), w_o[layer].astype(_BF16),
          w_norm_mlp[layer].reshape(1, d_model),
          w_mlp_in[layer].astype(_BF16).reshape(d_model, n_chunks, FF_CHUNK).transpose(1, 0, 2),
          w_mlp_out[layer].astype(_BF16))

    return xf.reshape(batch, seq, d_model)
```

```python
import functools

import jax
import jax.numpy as jnp
from jax import lax
from jax.experimental import pallas as pl
from jax.experimental.pallas import tpu as pltpu

D_MODEL = 1024
HEAD_DIM = 64
HEADS_PER_GROUP = 4
GROUP_WIDTH = HEADS_PER_GROUP * HEAD_DIM
DILATIONS = (1, 4, 16)
SPAN = 128
N_GROUPS = len(DILATIONS)
ATTN_WIDTH = N_GROUPS * GROUP_WIDTH
ROPE_THETA = 500000.0
ROPE_DIM = HEAD_DIM // 4
CONV_WIDTH = D_MODEL
CONV_KERNEL = 31
D_FF = 4 * D_MODEL
NORM_EPS = 1e-6
MASK_VALUE = -1e30
LOG2_E = 1.4426950408889634

LANES = 128
CONV_HALO = 32
ATTN_TILE = SPAN * DILATIONS[-1]
TOKEN_TILE = 512
CONV_ROWS = 32
CONV_STRIDE = 4
FF_CHUNK = 2048
COMBINE_ROWS = 256
VMEM_LIMIT = 56 * 1024 * 1024

_F32 = jnp.float32
_BF16 = jnp.bfloat16


def _dot(a, b):
    return jnp.dot(a, b, preferred_element_type=_F32)


def _sigmoid(z):
    return 0.5 * jnp.tanh(0.5 * z) + 0.5


def _rms_scale(xf):
    return lax.rsqrt(jnp.mean(xf * xf, axis=-1, keepdims=True) + NORM_EPS)


def _inproj_kernel(x_ref, pos_ref, wn_ref, w_ref, bg_ref, qkw_ref, freq_ref, sel_ref, aux_ref, seg_ref,
                   u_ref, q1_ref, q2_ref, q3_ref, k1_ref, k2_ref, k3_ref, v1_ref, v2_ref, v3_ref, g_ref,
                   z_ref):
    tm = x_ref.shape[0]
    x = x_ref[...]
    n = (x * _rms_scale(x) * wn_ref[...]).astype(_BF16)

    ang = freq_ref[...] * pos_ref[0]
    cs = jnp.concatenate([jnp.cos(ang), jnp.sin(ang)], axis=0).T
    cs_hi = cs.astype(_BF16)
    cs_lo = (cs - cs_hi.astype(_F32)).astype(_BF16)
    tables = _dot(cs_hi, sel_ref[...]) + _dot(cs_lo, sel_ref[...])
    cos = tables[:, 0:LANES] + aux_ref[0:1, :]
    sin = tables[:, LANES:2 * LANES]
    first_half = aux_ref[1:2, :] > 0.5
    seg = seg_ref[...]

    def store_by_residue(slot, out_ref, dil):
        for t in range(GROUP_WIDTH // LANES):
            for r in range(dil):
                rows = z_ref[slot, t] if dil == 1 else z_ref[slot, t, pl.ds(r, tm // dil, stride=dil), :]
                c = r * GROUP_WIDTH + t * LANES
                out_ref[:, c:c + LANES] = rows.astype(_BF16)

    def finish_qk(z, slot, gain_row, scale, out_ref, dil):
        ss = _dot((z * z).astype(_BF16), seg)
        z = z * lax.rsqrt(ss * (1.0 / HEAD_DIM) + NORM_EPS) * qkw_ref[gain_row:gain_row + 1, :]
        for t in range(GROUP_WIDTH // LANES):
            zt = z[:, t * LANES:(t + 1) * LANES]
            partner = jnp.where(first_half, pltpu.roll(zt, LANES - ROPE_DIM // 2, 1),
                                pltpu.roll(zt, ROPE_DIM // 2, 1))
            z_ref[slot, t] = (zt * cos + partner * sin) * scale
        store_by_residue(slot, out_ref, dil)

    def finish_v(z, slot, out_ref, dil):
        for t in range(GROUP_WIDTH // LANES):
            z_ref[slot, t] = z[:, t * LANES:(t + 1) * LANES]
        store_by_residue(slot, out_ref, dil)

    def finish_gate(z, t):
        z = z + bg_ref[:, t * D_MODEL:(t + 1) * D_MODEL]
        g_ref[:, t * D_MODEL:(t + 1) * D_MODEL] = _sigmoid(z).astype(_BF16)

    def finish_glu(z):
        u_ref[...] = (z[:, 0:CONV_WIDTH] * _sigmoid(z[:, CONV_WIDTH:2 * CONV_WIDTH])).astype(_BF16)

    c0 = 2 * CONV_WIDTH
    c3 = c0 + 3 * ATTN_WIDTH
    q_refs = (q1_ref, q2_ref, q3_ref)
    k_refs = (k1_ref, k2_ref, k3_ref)
    v_refs = (v1_ref, v2_ref, v3_ref)

    segments = [(c3, D_MODEL, functools.partial(finish_gate, t=0)),
                (c3 + D_MODEL, D_MODEL, functools.partial(finish_gate, t=1))]
    for g, dil in enumerate(DILATIONS):
        segments.append((c0 + g * GROUP_WIDTH, GROUP_WIDTH,
                         functools.partial(finish_qk, slot=0, gain_row=0, scale=HEAD_DIM ** -0.5 * LOG2_E,
                                           out_ref=q_refs[g], dil=dil)))
        segments.append((c0 + ATTN_WIDTH + g * GROUP_WIDTH, GROUP_WIDTH,
                         functools.partial(finish_qk, slot=1, gain_row=1, scale=1.0,
                                           out_ref=k_refs[g], dil=dil)))
    segments.append((0, 2 * CONV_WIDTH, finish_glu))
    for g, dil in enumerate(DILATIONS):
        segments.append((c0 + 2 * ATTN_WIDTH + g * GROUP_WIDTH, GROUP_WIDTH,
                         functools.partial(finish_v, slot=g % 2, out_ref=v_refs[g], dil=dil)))

    pending = None
    for col, width, finish in segments:
        z = _dot(n, w_ref[:, col:col + width])
        if pending is not None:
            pending()
        pending = functools.partial(finish, z)
    pending()


def _attn_kernel(q1_ref, k1_ref, k1p_ref, v1_ref, v1p_ref,
                 q2_ref, k2_ref, k2p_ref, v2_ref, v2p_ref,
                 q3_ref, k3_ref, k3p_ref, v3_ref, v3p_ref,
                 ow_ref,
                 o_scr, l_scr, m_scr):
    tile = pl.program_id(1)

    qi = lax.broadcasted_iota(jnp.int32, (SPAN, 2 * SPAN), 0)
    ki = lax.broadcasted_iota(jnp.int32, (SPAN, 2 * SPAN), 1)
    dist = qi + SPAN - ki
    band = jnp.where((dist >= 0) & (dist <= SPAN), 0.0, MASK_VALUE).astype(_F32)
    band_first = jnp.where((ki >= SPAN) | (tile > 0), band, MASK_VALUE)
    band2 = jnp.concatenate([band, band], axis=0)
    band_first2 = jnp.concatenate([band_first, band_first], axis=0)

    lane = lax.broadcasted_iota(jnp.int32, (SPAN, LANES), 1)
    low_head = lane < HEAD_DIM
    ones = jnp.ones((2 * SPAN, LANES), _BF16)

    def with_prev(cur_ref, prev_ref, j, cols):
        if j == 0:
            return jnp.concatenate([prev_ref[:, cols], cur_ref[0:SPAN, cols]], axis=0)
        return cur_ref[(j - 1) * SPAN:(j + 1) * SPAN, cols]

    def unit(refs, j, col0, tok_rows, out_tile0):
        q_ref, k_ref, kp_ref, v_ref, vp_ref = refs
        bias2 = band_first2 if j == 0 else band2
        for pair in range(GROUP_WIDTH // LANES):
            cols = slice(col0 + pair * LANES, col0 + (pair + 1) * LANES)
            q = q_ref[j * SPAN:(j + 1) * SPAN, cols]
            kk = with_prev(k_ref, kp_ref, j, cols)
            vv = with_prev(v_ref, vp_ref, j, cols)
            zero = jnp.zeros_like(q)
            q2 = jnp.concatenate([jnp.where(low_head, q, zero), jnp.where(low_head, zero, q)], axis=0)
            s = lax.dot_general(q2, kk, (((1,), (1,)), ((), ())), preferred_element_type=_F32) + bias2
            m = jnp.max(s, axis=-1, keepdims=True)
            p = jnp.exp2(s - m).astype(_BF16)
            o2 = _dot(p, jnp.concatenate([vv, ones], axis=1))
            m2 = jnp.broadcast_to(m, (2 * SPAN, LANES))
            ot = out_tile0 + pair
            o_scr[ot, tok_rows, :] = jnp.where(low_head, o2[0:SPAN, 0:LANES], o2[SPAN:2 * SPAN, 0:LANES])
            l_scr[ot, tok_rows, :] = jnp.where(low_head, o2[0:SPAN, LANES:2 * LANES],
                                               o2[SPAN:2 * SPAN, LANES:2 * LANES])
            m_scr[ot, tok_rows, :] = jnp.where(low_head, m2[0:SPAN], m2[SPAN:2 * SPAN])

    group_refs = ((q1_ref, k1_ref, k1p_ref, v1_ref, v1p_ref),
                  (q2_ref, k2_ref, k2p_ref, v2_ref, v2p_ref),
                  (q3_ref, k3_ref, k3p_ref, v3_ref, v3p_ref))
    for g, dil in enumerate(DILATIONS):
        for r in range(dil):
            for j in range(ATTN_TILE // dil // SPAN):
                tok_rows = pl.ds(dil * SPAN * j + r, SPAN) if dil == 1 else pl.ds(dil * SPAN * j + r, SPAN, stride=dil)
                unit(group_refs[g], j, r * GROUP_WIDTH, tok_rows, g * (GROUP_WIDTH // LANES))

    def combine(i, carry):
        rows = pl.ds(pl.multiple_of(i * COMBINE_ROWS, COMBINE_ROWS), COMBINE_ROWS)
        tiles_per_group = GROUP_WIDTH // LANES
        for t in range(tiles_per_group):
            tiles = [g * tiles_per_group + t for g in range(N_GROUPS)]
            ms = [m_scr[ot, rows, :] for ot in tiles]
            mx = jnp.maximum(jnp.maximum(ms[0], ms[1]), ms[2])
            ws = [jnp.exp2(m - mx) for m in ms]
            den = sum(l_scr[ot, rows, :] * w for ot, w in zip(tiles, ws))
            inv = 1.0 / den
            for ot, w in zip(tiles, ws):
                ow_ref[rows, ot * LANES:(ot + 1) * LANES] = (o_scr[ot, rows, :] * (w * inv)).astype(_BF16)
        return carry

    lax.fori_loop(0, ATTN_TILE // COMBINE_ROWS, combine, 0)


def _mix_mlp_kernel(x_ref, u_ref, uh_ref, ow_ref, g_ref,
                    wdw_ref, bdw_ref, wcn_ref, wco_ref, wao_ref, wo_ref, wnm_ref, w1_ref, w2_ref,
                    out_ref, ext_ref, y_ref, x1_ref, hn_ref, *, tiles_per_seq, n_tiles):
    step = pl.program_id(0)
    tile = jnp.minimum(step, n_tiles - 1)
    first = (tile % tiles_per_seq) == 0
    tm = x_ref.shape[0]
    cur = step % 2
    prev = 1 - cur

    @pl.when(step == 0)
    def _():
        x1_ref[1] = jnp.zeros((tm, D_MODEL), _F32)

    shift = CONV_HALO - (CONV_KERNEL - 1)
    block = CONV_ROWS * CONV_STRIDE
    n_lane_tiles = CONV_WIDTH // LANES
    n_chunks = D_FF // FF_CHUNK
    lane_tiles_per_chunk = n_lane_tiles // n_chunks
    assert lane_tiles_per_chunk * n_chunks == n_lane_tiles

    for c in range(n_lane_tiles):
        cols = slice(c * LANES, (c + 1) * LANES)
        halo = uh_ref[:, cols].astype(_F32)
        ext_ref[c, 0:CONV_HALO, :] = jnp.where(first, jnp.zeros_like(halo), halo)
        ext_ref[c, CONV_HALO:, :] = u_ref[:, cols].astype(_F32)

    def conv_lane_tile(c):
        taps = [jnp.broadcast_to(wdw_ref[c, k:k + 1, :], (CONV_ROWS, LANES)) for k in range(CONV_KERNEL)]
        bias = jnp.broadcast_to(bdw_ref[c], (CONV_ROWS, LANES))

        for base in range(0, tm, block):
            accs = [bias] * CONV_STRIDE
            for s in range(CONV_KERNEL + CONV_STRIDE - 1):
                rows = ext_ref[c, pl.ds(base + shift + s, CONV_ROWS, stride=CONV_STRIDE), :]
                for phase in range(CONV_STRIDE):
                    if 0 <= s - phase < CONV_KERNEL:
                        accs[phase] = accs[phase] + rows * taps[s - phase]
            for phase in range(CONV_STRIDE):
                y_ref[c, pl.ds(base + phase, CONV_ROWS, stride=CONV_STRIDE), :] = accs[phase]

    xp = x1_ref[prev]
    hn_ref[...] = (xp * _rms_scale(xp) * wnm_ref[...]).astype(_BF16)
    out_ref[...] = xp

    def chunk_body(c, carry):
        h = jnp.maximum(_dot(hn_ref[...], w1_ref[c]), 0.0)
        rows = pl.ds(pl.multiple_of(c * FF_CHUNK, FF_CHUNK), FF_CHUNK)
        out_ref[...] += _dot((h * h).astype(_BF16), w2_ref[rows, :])
        for t in range(lane_tiles_per_chunk):
            conv_lane_tile(c * lane_tiles_per_chunk + t)
        return carry

    lax.fori_loop(0, n_chunks, chunk_body, 0)

    y = jnp.concatenate([y_ref[c] for c in range(n_lane_tiles)], axis=1)
    y = y * _rms_scale(y) * wcn_ref[...]
    y = y * _sigmoid(y)
    branch_a = _dot(y.astype(_BF16), wco_ref[...])
    branch_b = _dot(ow_ref[...], wao_ref[...])
    merged = (g_ref[:, 0:D_MODEL].astype(_F32) * branch_a
              + g_ref[:, D_MODEL:2 * D_MODEL].astype(_F32) * branch_b)
    x1_ref[cur] = x_ref[...] + _dot(merged.astype(_BF16), wo_ref[...])


def _const_spec(shape):
    return pl.BlockSpec(shape, lambda *_: (0,) * len(shape))


def _weight_spec(shape):
    return pl.BlockSpec(shape, lambda *_: (0,) * len(shape), pipeline_mode=pl.Buffered(1))


def _rope_constants(tm):
    half = ROPE_DIM // 2
    inv_freq = ROPE_THETA ** (-jnp.arange(0, ROPE_DIM, 2, dtype=_F32) / ROPE_DIM)
    freq = jnp.broadcast_to(inv_freq[:, None], (half, tm))
    d = jnp.arange(LANES) % HEAD_DIM
    rotary = d < ROPE_DIM
    hit = (rotary[None, :] & ((d % half)[None, :] == jnp.arange(half)[:, None])).astype(_F32)
    sign = jnp.where(d < half, -1.0, 1.0)[None, :]
    zero = jnp.zeros_like(hit)
    sel = jnp.concatenate([jnp.concatenate([hit, zero], axis=1),
                           jnp.concatenate([zero, hit * sign], axis=1)], axis=0).astype(_BF16)
    aux = jnp.stack([1.0 - rotary.astype(_F32), (d < half).astype(_F32)]
                    + [jnp.zeros((LANES,), _F32)] * 6)
    return freq, sel, aux


def _params(n_axes):
    return pltpu.CompilerParams(dimension_semantics=("arbitrary",) * n_axes, vmem_limit_bytes=VMEM_LIMIT)


def kernel(x, positions, w_norm_mix, w_in, b_gate, w_dw, b_dw, w_conv_norm, w_conv_out, q_norm_w, k_norm_w,
           w_attn_out, w_o, w_norm_mlp, w_mlp_in, w_mlp_out):
    batch, seq, d_model = x.shape
    depth = w_in.shape[0]
    assert d_model == D_MODEL and seq % ATTN_TILE == 0 and ATTN_TILE % TOKEN_TILE == 0
    n_tok = batch * seq
    tm = TOKEN_TILE
    n_tiles = n_tok // tm
    in_width = w_in.shape[-1]

    xf = x.reshape(n_tok, d_model)
    pos = positions.astype(_F32).reshape(n_tiles, 1, tm)
    freq, sel, aux = _rope_constants(tm)
    head = jnp.arange(GROUP_WIDTH) // HEAD_DIM
    seg = (head[:, None] == head[None, :]).astype(_BF16)

    def row_spec(width):
        return pl.BlockSpec((tm, width), lambda i: (i, 0))

    for layer in range(depth):
        qk_gain = jnp.stack([jnp.tile(q_norm_w[layer], GROUP_WIDTH // HEAD_DIM),
                             jnp.tile(k_norm_w[layer], GROUP_WIDTH // HEAD_DIM)]).astype(_F32)

        def grp_spec(dil):
            return pl.BlockSpec((tm // dil, dil * GROUP_WIDTH), lambda i: (i, 0))

        grp_specs = [grp_spec(dil) for dil in DILATIONS] * 3
        grp_shapes = [jax.ShapeDtypeStruct((n_tok // dil, dil * GROUP_WIDTH), _BF16) for dil in DILATIONS] * 3
        outs = pl.pallas_call(
            _inproj_kernel,
            grid=(n_tiles,),
            in_specs=[row_spec(d_model), pl.BlockSpec((1, 1, tm), lambda i: (i, 0, 0)),
                      _const_spec((1, d_model)),
                      _const_spec((d_model, in_width)), _const_spec((1, 2 * d_model)),
                      _const_spec((2, GROUP_WIDTH)), _const_spec((ROPE_DIM // 2, tm)),
                      _const_spec((ROPE_DIM, 2 * LANES)), _const_spec((8, LANES)),
                      _const_spec((GROUP_WIDTH, GROUP_WIDTH))],
            out_specs=[row_spec(CONV_WIDTH)] + grp_specs + [row_spec(2 * d_model)],
            out_shape=[jax.ShapeDtypeStruct((n_tok, CONV_WIDTH), _BF16)] + grp_shapes
                      + [jax.ShapeDtypeStruct((n_tok, 2 * d_model), _BF16)],
            scratch_shapes=[pltpu.VMEM((2, GROUP_WIDTH // LANES, tm, LANES), _F32)],
            compiler_params=_params(1),
            name="inproj",
        )(xf, pos, w_norm_mix[layer].reshape(1, d_model), w_in[layer].astype(_BF16),
          b_gate[layer].reshape(1, 2 * d_model), qk_gain, freq, sel, aux, seg)
        u, qkv, gates = outs[0], outs[1:10], outs[10]

        n_attn_tiles = seq // ATTN_TILE
        in_specs, operands = [], []
        for g, dil in enumerate(DILATIONS):
            rows, width = ATTN_TILE // dil, dil * GROUP_WIDTH
            spans_per_tile, spans_per_seq = rows // SPAN, seq // dil // SPAN
            cur = pl.BlockSpec((rows, width), lambda b, i: (b * n_attn_tiles + i, 0))
            prev = pl.BlockSpec(
                (SPAN, width),
                lambda b, i, n=spans_per_tile, s=spans_per_seq: (b * s + jnp.maximum(i * n - 1, 0), 0))
            in_specs += [cur, cur, prev, cur, prev]
            operands += [qkv[g], qkv[3 + g], qkv[3 + g], qkv[6 + g], qkv[6 + g]]
        scratch = [pltpu.VMEM((ATTN_WIDTH // LANES, ATTN_TILE, LANES), _F32)] * 3
        ow = pl.pallas_call(
            _attn_kernel,
            grid=(batch, n_attn_tiles),
            in_specs=in_specs,
            out_specs=pl.BlockSpec((ATTN_TILE, ATTN_WIDTH), lambda b, i: (b * n_attn_tiles + i, 0)),
            out_shape=jax.ShapeDtypeStruct((n_tok, ATTN_WIDTH), _BF16),
            scratch_shapes=scratch,
            compiler_params=_params(2),
            name="attention",
        )(*operands)

        n_lane_tiles, n_chunks = CONV_WIDTH // LANES, D_FF // FF_CHUNK

        def tile_spec(width):
            return pl.BlockSpec((tm, width), lambda i: (jnp.minimum(i, n_tiles - 1), 0))

        halo_spec = pl.BlockSpec(
            (CONV_HALO, CONV_WIDTH),
            lambda i: (jnp.maximum(jnp.minimum(i, n_tiles - 1) * (tm // CONV_HALO) - 1, 0), 0))
        xf = pl.pallas_call(
            functools.partial(_mix_mlp_kernel, tiles_per_seq=seq // tm, n_tiles=n_tiles),
            grid=(n_tiles + 1,),
            in_specs=[tile_spec(d_model), tile_spec(CONV_WIDTH), halo_spec, tile_spec(ATTN_WIDTH),
                      tile_spec(2 * d_model),
                      _const_spec((n_lane_tiles, CONV_KERNEL, LANES)), _const_spec((n_lane_tiles, 1, LANES)),
                      _const_spec((1, CONV_WIDTH)), _weight_spec((CONV_WIDTH, d_model)),
                      _weight_spec((ATTN_WIDTH, d_model)), _weight_spec((d_model, d_model)),
                      _const_spec((1, d_model)), _weight_spec((n_chunks, d_model, FF_CHUNK)),
                      _weight_spec((D_FF, d_model))],
            out_specs=pl.BlockSpec((tm, d_model), lambda i: (jnp.maximum(i - 1, 0), 0)),
            out_shape=jax.ShapeDtypeStruct((n_tok, d_model), _F32),
            scratch_shapes=[pltpu.VMEM((CONV_WIDTH // LANES, CONV_HALO + tm, LANES), _F32),
                            pltpu.VMEM((CONV_WIDTH // LANES, tm, LANES), _F32),
                            pltpu.VMEM((2, tm, d_model), _F32),
                            pltpu.VMEM((tm, d_model), _BF16)],
            compiler_params=_params(1),
            name="mixer_mlp",
        )(xf, u, u, ow, gates,
          w_dw[layer].reshape(CONV_KERNEL, n_lane_tiles, LANES).transpose(1, 0, 2),
          b_dw[layer].reshape(n_lane_tiles, 1, LANES),
          w_conv_norm[layer].reshape(1, CONV_WIDTH), w_conv_out[layer].astype(_BF16),
          w_attn_out[layer].astype(_BF16), w_o[layer].astype(_BF16),
          w_norm_mlp[layer].reshape(1, d_model),
          w_mlp_in[layer].astype(_BF16).reshape(d_model, n_chunks, FF_CHUNK).transpose(1, 0, 2),
          w_mlp_out[layer].astype(_BF16))

    return xf.reshape(batch, seq, d_model)
```

```python
import functools

import jax
import jax.numpy as jnp
from jax import lax
from jax.experimental import pallas as pl
from jax.experimental.pallas import tpu as pltpu

D_MODEL = 1024
HEAD_DIM = 64
HEADS_PER_GROUP = 4
GROUP_WIDTH = HEADS_PER_GROUP * HEAD_DIM
DILATIONS = (1, 4, 16)
SPAN = 128
N_GROUPS = len(DILATIONS)
ATTN_WIDTH = N_GROUPS * GROUP_WIDTH
ROPE_THETA = 500000.0
ROPE_DIM = HEAD_DIM // 4
CONV_WIDTH = D_MODEL
CONV_KERNEL = 31
D_FF = 4 * D_MODEL
NORM_EPS = 1e-6
MASK_VALUE = -1e30
LOG2_E = 1.4426950408889634

LANES = 128
CONV_HALO = 32
ATTN_TILE = SPAN * DILATIONS[-1]
TOKEN_TILE = 512
CONV_ROWS = 32
CONV_STRIDE = 4
FF_CHUNK = 2048
COMBINE_ROWS = 256
VMEM_LIMIT = 56 * 1024 * 1024

_F32 = jnp.float32
_BF16 = jnp.bfloat16


def _dot(a, b):
    return jnp.dot(a, b, preferred_element_type=_F32)


def _sigmoid(z):
    return 0.5 * jnp.tanh(0.5 * z) + 0.5


def _rms_scale(xf):
    return lax.rsqrt(jnp.mean(xf * xf, axis=-1, keepdims=True) + NORM_EPS)


def _inproj_kernel(x_ref, pos_ref, wn_ref, w_ref, bg_ref, qkw_ref, freq_ref, sel_ref, aux_ref, seg_ref,
                   u_ref, q1_ref, q2_ref, q3_ref, k1_ref, k2_ref, k3_ref, v1_ref, v2_ref, v3_ref, g_ref,
                   z_ref):
    tm = x_ref.shape[0]
    x = x_ref[...]
    n = (x * wn_ref[...]).astype(_BF16)
    r = _rms_scale(x)

    ang = freq_ref[...] * pos_ref[0]
    cs = jnp.concatenate([jnp.cos(ang), jnp.sin(ang)], axis=0).T
    cs_hi = cs.astype(_BF16)
    cs_lo = (cs - cs_hi.astype(_F32)).astype(_BF16)
    tables = _dot(cs_hi, sel_ref[...]) + _dot(cs_lo, sel_ref[...])
    cos = tables[:, 0:LANES] + aux_ref[0:1, :]
    sin = tables[:, LANES:2 * LANES]
    first_half = aux_ref[1:2, :] > 0.5
    seg = seg_ref[...]

    def store_by_residue(slot, out_ref, dil):
        for t in range(GROUP_WIDTH // LANES):
            for r in range(dil):
                rows = z_ref[slot, t] if dil == 1 else z_ref[slot, t, pl.ds(r, tm // dil, stride=dil), :]
                c = r * GROUP_WIDTH + t * LANES
                out_ref[:, c:c + LANES] = rows.astype(_BF16)

    def finish_qk(z, slot, gain_row, scale, out_ref, dil):
        z = z * r
        ss = _dot((z * z).astype(_BF16), seg)
        z = z * lax.rsqrt(ss * (1.0 / HEAD_DIM) + NORM_EPS) * qkw_ref[gain_row:gain_row + 1, :]
        for t in range(GROUP_WIDTH // LANES):
            zt = z[:, t * LANES:(t + 1) * LANES]
            partner = jnp.where(first_half, pltpu.roll(zt, LANES - ROPE_DIM // 2, 1),
                                pltpu.roll(zt, ROPE_DIM // 2, 1))
            z_ref[slot, t] = (zt * cos + partner * sin) * scale
        store_by_residue(slot, out_ref, dil)

    def finish_v(z, slot, out_ref, dil):
        for t in range(GROUP_WIDTH // LANES):
            z_ref[slot, t] = z[:, t * LANES:(t + 1) * LANES] * r
        store_by_residue(slot, out_ref, dil)

    def finish_gate(z, t):
        z = z * r + bg_ref[:, t * D_MODEL:(t + 1) * D_MODEL]
        g_ref[:, t * D_MODEL:(t + 1) * D_MODEL] = _sigmoid(z).astype(_BF16)

    def finish_glu(z):
        z = z * r
        u_ref[...] = (z[:, 0:CONV_WIDTH] * _sigmoid(z[:, CONV_WIDTH:2 * CONV_WIDTH])).astype(_BF16)

    c0 = 2 * CONV_WIDTH
    c3 = c0 + 3 * ATTN_WIDTH
    q_refs = (q1_ref, q2_ref, q3_ref)
    k_refs = (k1_ref, k2_ref, k3_ref)
    v_refs = (v1_ref, v2_ref, v3_ref)

    segments = [(c3, D_MODEL, functools.partial(finish_gate, t=0)),
                (c3 + D_MODEL, D_MODEL, functools.partial(finish_gate, t=1))]
    for g, dil in enumerate(DILATIONS):
        segments.append((c0 + g * GROUP_WIDTH, GROUP_WIDTH,
                         functools.partial(finish_qk, slot=0, gain_row=0, scale=HEAD_DIM ** -0.5 * LOG2_E,
                                           out_ref=q_refs[g], dil=dil)))
        segments.append((c0 + ATTN_WIDTH + g * GROUP_WIDTH, GROUP_WIDTH,
                         functools.partial(finish_qk, slot=1, gain_row=1, scale=1.0,
                                           out_ref=k_refs[g], dil=dil)))
    segments.append((0, 2 * CONV_WIDTH, finish_glu))
    for g, dil in enumerate(DILATIONS):
        segments.append((c0 + 2 * ATTN_WIDTH + g * GROUP_WIDTH, GROUP_WIDTH,
                         functools.partial(finish_v, slot=g % 2, out_ref=v_refs[g], dil=dil)))

    pending = None
    for col, width, finish in segments:
        z = _dot(n, w_ref[:, col:col + width])
        if pending is not None:
            pending()
        pending = functools.partial(finish, z)
    pending()


def _attn_kernel(q1_ref, k1_ref, k1p_ref, v1_ref, v1p_ref,
                 q2_ref, k2_ref, k2p_ref, v2_ref, v2p_ref,
                 q3_ref, k3_ref, k3p_ref, v3_ref, v3p_ref,
                 ow_ref,
                 o_scr, l_scr, m_scr):
    tile = pl.program_id(1)

    qi = lax.broadcasted_iota(jnp.int32, (SPAN, 2 * SPAN), 0)
    ki = lax.broadcasted_iota(jnp.int32, (SPAN, 2 * SPAN), 1)
    dist = qi + SPAN - ki
    band = jnp.where((dist >= 0) & (dist <= SPAN), 0.0, MASK_VALUE).astype(_F32)
    band_first = jnp.where((ki >= SPAN) | (tile > 0), band, MASK_VALUE)
    band2 = jnp.concatenate([band, band], axis=0)
    band_first2 = jnp.concatenate([band_first, band_first], axis=0)

    lane = lax.broadcasted_iota(jnp.int32, (SPAN, LANES), 1)
    low_head = lane < HEAD_DIM
    ones = jnp.ones((2 * SPAN, LANES), _BF16)

    def with_prev(cur_ref, prev_ref, j, cols):
        if j == 0:
            return jnp.concatenate([prev_ref[:, cols], cur_ref[0:SPAN, cols]], axis=0)
        return cur_ref[(j - 1) * SPAN:(j + 1) * SPAN, cols]

    def unit(refs, j, col0, tok_rows, out_tile0):
        q_ref, k_ref, kp_ref, v_ref, vp_ref = refs
        bias2 = band_first2 if j == 0 else band2
        for pair in range(GROUP_WIDTH // LANES):
            cols = slice(col0 + pair * LANES, col0 + (pair + 1) * LANES)
            q = q_ref[j * SPAN:(j + 1) * SPAN, cols]
            kk = with_prev(k_ref, kp_ref, j, cols)
            vv = with_prev(v_ref, vp_ref, j, cols)
            zero = jnp.zeros_like(q)
            q2 = jnp.concatenate([jnp.where(low_head, q, zero), jnp.where(low_head, zero, q)], axis=0)
            s = lax.dot_general(q2, kk, (((1,), (1,)), ((), ())), preferred_element_type=_F32) + bias2
            m = jnp.max(s, axis=-1, keepdims=True)
            p = jnp.exp2(s - m).astype(_BF16)
            o2 = _dot(p, jnp.concatenate([vv, ones], axis=1))
            m2 = jnp.broadcast_to(m, (2 * SPAN, LANES))
            ot = out_tile0 + pair
            o_scr[ot, tok_rows, :] = jnp.where(low_head, o2[0:SPAN, 0:LANES], o2[SPAN:2 * SPAN, 0:LANES])
            l_scr[ot, tok_rows, :] = jnp.where(low_head, o2[0:SPAN, LANES:2 * LANES],
                                               o2[SPAN:2 * SPAN, LANES:2 * LANES])
            m_scr[ot, tok_rows, :] = jnp.where(low_head, m2[0:SPAN], m2[SPAN:2 * SPAN])

    group_refs = ((q1_ref, k1_ref, k1p_ref, v1_ref, v1p_ref),
                  (q2_ref, k2_ref, k2p_ref, v2_ref, v2p_ref),
                  (q3_ref, k3_ref, k3p_ref, v3_ref, v3p_ref))
    blocks_per_group = ATTN_TILE // SPAN
    for b in range(blocks_per_group):
        for g, dil in enumerate(DILATIONS):
            r, j = divmod(b, ATTN_TILE // dil // SPAN)
            tok_rows = pl.ds(dil * SPAN * j + r, SPAN) if dil == 1 else pl.ds(dil * SPAN * j + r, SPAN, stride=dil)
            unit(group_refs[g], j, r * GROUP_WIDTH, tok_rows, g * (GROUP_WIDTH // LANES))

    def combine(i, carry):
        rows = pl.ds(pl.multiple_of(i * COMBINE_ROWS, COMBINE_ROWS), COMBINE_ROWS)
        tiles_per_group = GROUP_WIDTH // LANES
        for t in range(tiles_per_group):
            tiles = [g * tiles_per_group + t for g in range(N_GROUPS)]
            ms = [m_scr[ot, rows, :] for ot in tiles]
            mx = jnp.maximum(jnp.maximum(ms[0], ms[1]), ms[2])
            ws = [jnp.exp2(m - mx) for m in ms]
            den = sum(l_scr[ot, rows, :] * w for ot, w in zip(tiles, ws))
            inv = 1.0 / den
            for ot, w in zip(tiles, ws):
                ow_ref[rows, ot * LANES:(ot + 1) * LANES] = (o_scr[ot, rows, :] * (w * inv)).astype(_BF16)
        return carry

    lax.fori_loop(0, ATTN_TILE // COMBINE_ROWS, combine, 0)


def _mix_mlp_kernel(x_ref, u_ref, uh_ref, ow_ref, g_ref,
                    wdw_ref, bdw_ref, wcn_ref, wco_ref, wao_ref, wo_ref, wnm_ref, w1_ref, w2_ref,
                    out_ref, ext_ref, y_ref, x1_ref, hn_ref, *, tiles_per_seq, n_tiles):
    step = pl.program_id(0)
    tile = jnp.minimum(step, n_tiles - 1)
    first = (tile % tiles_per_seq) == 0
    tm = x_ref.shape[0]
    cur = step % 2
    prev = 1 - cur

    @pl.when(step == 0)
    def _():
        x1_ref[1] = jnp.zeros((tm, D_MODEL), _F32)

    shift = CONV_HALO - (CONV_KERNEL - 1)
    block = CONV_ROWS * CONV_STRIDE
    n_lane_tiles = CONV_WIDTH // LANES
    n_chunks = D_FF // FF_CHUNK
    lane_tiles_per_chunk = n_lane_tiles // n_chunks
    assert lane_tiles_per_chunk * n_chunks == n_lane_tiles

    for c in range(n_lane_tiles):
        cols = slice(c * LANES, (c + 1) * LANES)
        halo = uh_ref[:, cols].astype(_F32)
        ext_ref[c, 0:CONV_HALO, :] = jnp.where(first, jnp.zeros_like(halo), halo)
        ext_ref[c, CONV_HALO:, :] = u_ref[:, cols].astype(_F32)

    def conv_lane_tile(c):
        taps = [jnp.broadcast_to(wdw_ref[c, k:k + 1, :], (CONV_ROWS, LANES)) for k in range(CONV_KERNEL)]
        bias = jnp.broadcast_to(bdw_ref[c], (CONV_ROWS, LANES))

        for base in range(0, tm, block):
            accs = [bias] * CONV_STRIDE
            for s in range(CONV_KERNEL + CONV_STRIDE - 1):
                rows = ext_ref[c, pl.ds(base + shift + s, CONV_ROWS, stride=CONV_STRIDE), :]
                for phase in range(CONV_STRIDE):
                    if 0 <= s - phase < CONV_KERNEL:
                        accs[phase] = accs[phase] + rows * taps[s - phase]
            for phase in range(CONV_STRIDE):
                y_ref[c, pl.ds(base + phase, CONV_ROWS, stride=CONV_STRIDE), :] = accs[phase]

    xp = x1_ref[prev]
    hn_ref[...] = (xp * _rms_scale(xp) * wnm_ref[...]).astype(_BF16)
    out_ref[...] = xp

    def chunk_body(c, carry):
        h = jnp.maximum(_dot(hn_ref[...], w1_ref[c]), 0.0)
        rows = pl.ds(pl.multiple_of(c * FF_CHUNK, FF_CHUNK), FF_CHUNK)
        out_ref[...] += _dot((h * h).astype(_BF16), w2_ref[rows, :])
        for t in range(lane_tiles_per_chunk):
            conv_lane_tile(c * lane_tiles_per_chunk + t)
        return carry

    lax.fori_loop(0, n_chunks, chunk_body, 0)

    branch_b = _dot(ow_ref[...], wao_ref[...])
    y = jnp.concatenate([y_ref[c] for c in range(n_lane_tiles)], axis=1)
    y = y * _rms_scale(y) * wcn_ref[...]
    y = y * _sigmoid(y)
    branch_a = _dot(y.astype(_BF16), wco_ref[...])
    merged = (g_ref[:, 0:D_MODEL].astype(_F32) * branch_a
              + g_ref[:, D_MODEL:2 * D_MODEL].astype(_F32) * branch_b)
    x1_ref[cur] = x_ref[...] + _dot(merged.astype(_BF16), wo_ref[...])


def _const_spec(shape):
    return pl.BlockSpec(shape, lambda *_: (0,) * len(shape))


def _weight_spec(shape):
    return pl.BlockSpec(shape, lambda *_: (0,) * len(shape), pipeline_mode=pl.Buffered(1))


def _rope_constants(tm):
    half = ROPE_DIM // 2
    inv_freq = ROPE_THETA ** (-jnp.arange(0, ROPE_DIM, 2, dtype=_F32) / ROPE_DIM)
    freq = jnp.broadcast_to(inv_freq[:, None], (half, tm))
    d = jnp.arange(LANES) % HEAD_DIM
    rotary = d < ROPE_DIM
    hit = (rotary[None, :] & ((d % half)[None, :] == jnp.arange(half)[:, None])).astype(_F32)
    sign = jnp.where(d < half, -1.0, 1.0)[None, :]
    zero = jnp.zeros_like(hit)
    sel = jnp.concatenate([jnp.concatenate([hit, zero], axis=1),
                           jnp.concatenate([zero, hit * sign], axis=1)], axis=0).astype(_BF16)
    aux = jnp.stack([1.0 - rotary.astype(_F32), (d < half).astype(_F32)]
                    + [jnp.zeros((LANES,), _F32)] * 6)
    return freq, sel, aux


def _params(n_axes):
    return pltpu.CompilerParams(dimension_semantics=("arbitrary",) * n_axes, vmem_limit_bytes=VMEM_LIMIT)


def kernel(x, positions, w_norm_mix, w_in, b_gate, w_dw, b_dw, w_conv_norm, w_conv_out, q_norm_w, k_norm_w,
           w_attn_out, w_o, w_norm_mlp, w_mlp_in, w_mlp_out):
    batch, seq, d_model = x.shape
    depth = w_in.shape[0]
    assert d_model == D_MODEL and seq % ATTN_TILE == 0 and ATTN_TILE % TOKEN_TILE == 0
    n_tok = batch * seq
    tm = TOKEN_TILE
    n_tiles = n_tok // tm
    in_width = w_in.shape[-1]

    xf = x.reshape(n_tok, d_model)
    pos = positions.astype(_F32).reshape(n_tiles, 1, tm)
    freq, sel, aux = _rope_constants(tm)
    head = jnp.arange(GROUP_WIDTH) // HEAD_DIM
    seg = (head[:, None] == head[None, :]).astype(_BF16)

    def row_spec(width):
        return pl.BlockSpec((tm, width), lambda i: (i, 0))

    for layer in range(depth):
        qk_gain = jnp.stack([jnp.tile(q_norm_w[layer], GROUP_WIDTH // HEAD_DIM),
                             jnp.tile(k_norm_w[layer], GROUP_WIDTH // HEAD_DIM)]).astype(_F32)

        def grp_spec(dil):
            return pl.BlockSpec((tm // dil, dil * GROUP_WIDTH), lambda i: (i, 0))

        grp_specs = [grp_spec(dil) for dil in DILATIONS] * 3
        grp_shapes = [jax.ShapeDtypeStruct((n_tok // dil, dil * GROUP_WIDTH), _BF16) for dil in DILATIONS] * 3
        outs = pl.pallas_call(
            _inproj_kernel,
            grid=(n_tiles,),
            in_specs=[row_spec(d_model), pl.BlockSpec((1, 1, tm), lambda i: (i, 0, 0)),
                      _const_spec((1, d_model)),
                      _const_spec((d_model, in_width)), _const_spec((1, 2 * d_model)),
                      _const_spec((2, GROUP_WIDTH)), _const_spec((ROPE_DIM // 2, tm)),
                      _const_spec((ROPE_DIM, 2 * LANES)), _const_spec((8, LANES)),
                      _const_spec((GROUP_WIDTH, GROUP_WIDTH))],
            out_specs=[row_spec(CONV_WIDTH)] + grp_specs + [row_spec(2 * d_model)],
            out_shape=[jax.ShapeDtypeStruct((n_tok, CONV_WIDTH), _BF16)] + grp_shapes
                      + [jax.ShapeDtypeStruct((n_tok, 2 * d_model), _BF16)],
            scratch_shapes=[pltpu.VMEM((2, GROUP_WIDTH // LANES, tm, LANES), _F32)],
            compiler_params=_params(1),
            name="inproj",
        )(xf, pos, w_norm_mix[layer].reshape(1, d_model), w_in[layer].astype(_BF16),
          b_gate[layer].reshape(1, 2 * d_model), qk_gain, freq, sel, aux, seg)
        u, qkv, gates = outs[0], outs[1:10], outs[10]

        n_attn_tiles = seq // ATTN_TILE
        in_specs, operands = [], []
        for g, dil in enumerate(DILATIONS):
            rows, width = ATTN_TILE // dil, dil * GROUP_WIDTH
            spans_per_tile, spans_per_seq = rows // SPAN, seq // dil // SPAN
            cur = pl.BlockSpec((rows, width), lambda b, i: (b * n_attn_tiles + i, 0))
            prev = pl.BlockSpec(
                (SPAN, width),
                lambda b, i, n=spans_per_tile, s=spans_per_seq: (b * s + jnp.maximum(i * n - 1, 0), 0))
            in_specs += [cur, cur, prev, cur, prev]
            operands += [qkv[g], qkv[3 + g], qkv[3 + g], qkv[6 + g], qkv[6 + g]]
        scratch = [pltpu.VMEM((ATTN_WIDTH // LANES, ATTN_TILE, LANES), _F32)] * 3
        ow = pl.pallas_call(
            _attn_kernel,
            grid=(batch, n_attn_tiles),
            in_specs=in_specs,
            out_specs=pl.BlockSpec((ATTN_TILE, ATTN_WIDTH), lambda b, i: (b * n_attn_tiles + i, 0)),
            out_shape=jax.ShapeDtypeStruct((n_tok, ATTN_WIDTH), _BF16),
            scratch_shapes=scratch,
            compiler_params=_params(2),
            name="attention",
        )(*operands)

        n_lane_tiles, n_chunks = CONV_WIDTH // LANES, D_FF // FF_CHUNK

        def tile_spec(width):
            return pl.BlockSpec((tm, width), lambda i: (jnp.minimum(i, n_tiles - 1), 0))

        halo_spec = pl.BlockSpec(
            (CONV_HALO, CONV_WIDTH),
            lambda i: (jnp.maximum(jnp.minimum(i, n_tiles - 1) * (tm // CONV_HALO) - 1, 0), 0))
        xf = pl.pallas_call(
            functools.partial(_mix_mlp_kernel, tiles_per_seq=seq // tm, n_tiles=n_tiles),
            grid=(n_tiles + 1,),
            in_specs=[tile_spec(d_model), tile_spec(CONV_WIDTH), halo_spec, tile_spec(ATTN_WIDTH),
                      tile_spec(2 * d_model),
                      _const_spec((n_lane_tiles, CONV_KERNEL, LANES)), _const_spec((n_lane_tiles, 1, LANES)),
                      _const_spec((1, CONV_WIDTH)), _weight_spec((CONV_WIDTH, d_model)),
                      _weight_spec((ATTN_WIDTH, d_model)), _weight_spec((d_model, d_model)),
                      _const_spec((1, d_model)), _weight_spec((n_chunks, d_model, FF_CHUNK)),
                      _weight_spec((D_FF, d_model))],
            out_specs=pl.BlockSpec((tm, d_model), lambda i: (jnp.maximum(i - 1, 0), 0)),
            out_shape=jax.ShapeDtypeStruct((n_tok, d_model), _F32),
            scratch_shapes=[pltpu.VMEM((CONV_WIDTH // LANES, CONV_HALO + tm, LANES), _F32),
                            pltpu.VMEM((CONV_WIDTH // LANES, tm, LANES), _F32),
                            pltpu.VMEM((2, tm, d_model), _F32),
                            pltpu.VMEM((tm, d_model), _BF16)],
            compiler_params=_params(1),
            name="mixer_mlp",
        )(xf, u, u, ow, gates,
          w_dw[layer].reshape(CONV_KERNEL, n_lane_tiles, LANES).transpose(1, 0, 2),
          b_dw[layer].reshape(n_lane_tiles, 1, LANES),
          w_conv_norm[layer].reshape(1, CONV_WIDTH), w_conv_out[layer].astype(_BF16),
          w_attn_out[layer].astype(_BF16), w_o[layer].astype(_BF16),
          w_norm_mlp[layer].reshape(1, d_model),
          w_mlp_in[layer].astype(_BF16).reshape(d_model, n_chunks, FF_CHUNK).transpose(1, 0, 2),
          w_mlp_out[layer].astype(_BF16))

    return xf.reshape(batch, seq, d_model)
```

```python
import functools

import jax
import jax.numpy as jnp
from jax import lax
from jax.experimental import pallas as pl
from jax.experimental.pallas import tpu as pltpu

D_MODEL = 1024
HEAD_DIM = 64
HEADS_PER_GROUP = 4
GROUP_WIDTH = HEADS_PER_GROUP * HEAD_DIM
DILATIONS = (1, 4, 16)
SPAN = 128
N_GROUPS = len(DILATIONS)
ATTN_WIDTH = N_GROUPS * GROUP_WIDTH
ROPE_THETA = 500000.0
ROPE_DIM = HEAD_DIM // 4
CONV_WIDTH = D_MODEL
CONV_KERNEL = 31
D_FF = 4 * D_MODEL
NORM_EPS = 1e-6
MASK_VALUE = -1e30
LOG2_E = 1.4426950408889634

LANES = 128
CONV_HALO = 32
ATTN_TILE = SPAN * DILATIONS[-1]
TOKEN_TILE = 512
CONV_ROWS = 32
CONV_STRIDE = 4
FF_CHUNK = 2048
COMBINE_ROWS = 256
VMEM_LIMIT = 56 * 1024 * 1024

_F32 = jnp.float32
_BF16 = jnp.bfloat16


def _dot(a, b):
    return jnp.dot(a, b, preferred_element_type=_F32)


def _sigmoid(z):
    return 0.5 * jnp.tanh(0.5 * z) + 0.5


def _rms_scale(xf):
    return lax.rsqrt(jnp.mean(xf * xf, axis=-1, keepdims=True) + NORM_EPS)


def _inproj_kernel(x_ref, pos_ref, wn_ref, w_ref, bg_ref, qkw_ref, freq_ref, sel_ref, aux_ref, seg_ref,
                   u_ref, q1_ref, q2_ref, q3_ref, k1_ref, k2_ref, k3_ref, v1_ref, v2_ref, v3_ref, g_ref,
                   z_ref):
    tm = x_ref.shape[0]
    x = x_ref[...]
    n = (x * wn_ref[...]).astype(_BF16)
    r = _rms_scale(x)

    ang = freq_ref[...] * pos_ref[0]
    cs = jnp.concatenate([jnp.cos(ang), jnp.sin(ang)], axis=0).T
    cs_hi = cs.astype(_BF16)
    cs_lo = (cs - cs_hi.astype(_F32)).astype(_BF16)
    tables = _dot(cs_hi, sel_ref[...]) + _dot(cs_lo, sel_ref[...])
    cos = tables[:, 0:LANES] + aux_ref[0:1, :]
    sin = tables[:, LANES:2 * LANES]
    first_half = aux_ref[1:2, :] > 0.5
    seg = seg_ref[...]

    def store_by_residue(slot, out_ref, dil):
        for t in range(GROUP_WIDTH // LANES):
            for r in range(dil):
                rows = z_ref[slot, t] if dil == 1 else z_ref[slot, t, pl.ds(r, tm // dil, stride=dil), :]
                c = r * GROUP_WIDTH + t * LANES
                out_ref[:, c:c + LANES] = rows.astype(_BF16)

    def finish_qk(z, slot, gain_row, scale, out_ref, dil):
        z = z * r
        ss = _dot((z * z).astype(_BF16), seg)
        z = z * lax.rsqrt(ss * (1.0 / HEAD_DIM) + NORM_EPS) * qkw_ref[gain_row:gain_row + 1, :]
        for t in range(GROUP_WIDTH // LANES):
            zt = z[:, t * LANES:(t + 1) * LANES]
            partner = jnp.where(first_half, pltpu.roll(zt, LANES - ROPE_DIM // 2, 1),
                                pltpu.roll(zt, ROPE_DIM // 2, 1))
            z_ref[slot, t] = (zt * cos + partner * sin) * scale
        store_by_residue(slot, out_ref, dil)

    def finish_v(z, slot, out_ref, dil):
        for t in range(GROUP_WIDTH // LANES):
            z_ref[slot, t] = z[:, t * LANES:(t + 1) * LANES] * r
        store_by_residue(slot, out_ref, dil)

    def finish_gate(z, t):
        z = z * r + bg_ref[:, t * D_MODEL:(t + 1) * D_MODEL]
        g_ref[:, t * D_MODEL:(t + 1) * D_MODEL] = _sigmoid(z).astype(_BF16)

    def finish_glu(z):
        z = z * r
        u_ref[...] = (z[:, 0:CONV_WIDTH] * _sigmoid(z[:, CONV_WIDTH:2 * CONV_WIDTH])).astype(_BF16)

    c0 = 2 * CONV_WIDTH
    c3 = c0 + 3 * ATTN_WIDTH
    q_refs = (q1_ref, q2_ref, q3_ref)
    k_refs = (k1_ref, k2_ref, k3_ref)
    v_refs = (v1_ref, v2_ref, v3_ref)

    segments = [(c3, D_MODEL, functools.partial(finish_gate, t=0)),
                (c3 + D_MODEL, D_MODEL, functools.partial(finish_gate, t=1))]
    for g, dil in enumerate(DILATIONS):
        segments.append((c0 + g * GROUP_WIDTH, GROUP_WIDTH,
                         functools.partial(finish_qk, slot=0, gain_row=0, scale=HEAD_DIM ** -0.5 * LOG2_E,
                                           out_ref=q_refs[g], dil=dil)))
        segments.append((c0 + ATTN_WIDTH + g * GROUP_WIDTH, GROUP_WIDTH,
                         functools.partial(finish_qk, slot=1, gain_row=1, scale=1.0,
                                           out_ref=k_refs[g], dil=dil)))
    segments.append((0, 2 * CONV_WIDTH, finish_glu))
    for g, dil in enumerate(DILATIONS):
        segments.append((c0 + 2 * ATTN_WIDTH + g * GROUP_WIDTH, GROUP_WIDTH,
                         functools.partial(finish_v, slot=g % 2, out_ref=v_refs[g], dil=dil)))

    pending = None
    for col, width, finish in segments:
        z = _dot(n, w_ref[:, col:col + width])
        if pending is not None:
            pending()
        pending = functools.partial(finish, z)
    pending()


def _attn_kernel(q1_ref, k1_ref, k1p_ref, v1_ref, v1p_ref,
                 q2_ref, k2_ref, k2p_ref, v2_ref, v2p_ref,
                 q3_ref, k3_ref, k3p_ref, v3_ref, v3p_ref,
                 ow_ref,
                 o_scr, l_scr, m_scr):
    tile = pl.program_id(1)

    qi = lax.broadcasted_iota(jnp.int32, (SPAN, 2 * SPAN), 0)
    ki = lax.broadcasted_iota(jnp.int32, (SPAN, 2 * SPAN), 1)
    dist = qi + SPAN - ki
    band = jnp.where((dist >= 0) & (dist <= SPAN), 0.0, MASK_VALUE).astype(_F32)
    band_first = jnp.where((ki >= SPAN) | (tile > 0), band, MASK_VALUE)
    band2 = jnp.concatenate([band, band], axis=0)
    band_first2 = jnp.concatenate([band_first, band_first], axis=0)

    lane = lax.broadcasted_iota(jnp.int32, (SPAN, LANES), 1)
    low_head = lane < HEAD_DIM
    ones = jnp.ones((2 * SPAN, LANES), _BF16)

    def with_prev(cur_ref, prev_ref, j, cols):
        if j == 0:
            return jnp.concatenate([prev_ref[:, cols], cur_ref[0:SPAN, cols]], axis=0)
        return cur_ref[(j - 1) * SPAN:(j + 1) * SPAN, cols]

    def unit(refs, j, col0, tok_rows, out_tile0):
        q_ref, k_ref, kp_ref, v_ref, vp_ref = refs
        bias2 = band_first2 if j == 0 else band2
        for pair in range(GROUP_WIDTH // LANES):
            cols = slice(col0 + pair * LANES, col0 + (pair + 1) * LANES)
            q = q_ref[j * SPAN:(j + 1) * SPAN, cols]
            kk = with_prev(k_ref, kp_ref, j, cols)
            vv = with_prev(v_ref, vp_ref, j, cols)
            zero = jnp.zeros_like(q)
            q2 = jnp.concatenate([jnp.where(low_head, q, zero), jnp.where(low_head, zero, q)], axis=0)
            s = lax.dot_general(q2, kk, (((1,), (1,)), ((), ())), preferred_element_type=_F32) + bias2
            m = jnp.max(s, axis=-1, keepdims=True)
            p = jnp.exp2(s - m).astype(_BF16)
            o2 = _dot(p, jnp.concatenate([vv, ones], axis=1))
            m2 = jnp.broadcast_to(m, (2 * SPAN, LANES))
            ot = out_tile0 + pair
            o_scr[ot, tok_rows, :] = jnp.where(low_head, o2[0:SPAN, 0:LANES], o2[SPAN:2 * SPAN, 0:LANES])
            l_scr[ot, tok_rows, :] = jnp.where(low_head, o2[0:SPAN, LANES:2 * LANES],
                                               o2[SPAN:2 * SPAN, LANES:2 * LANES])
            m_scr[ot, tok_rows, :] = jnp.where(low_head, m2[0:SPAN], m2[SPAN:2 * SPAN])

    group_refs = ((q1_ref, k1_ref, k1p_ref, v1_ref, v1p_ref),
                  (q2_ref, k2_ref, k2p_ref, v2_ref, v2p_ref),
                  (q3_ref, k3_ref, k3p_ref, v3_ref, v3p_ref))
    for g, dil in enumerate(DILATIONS):
        for r in range(dil):
            for j in range(ATTN_TILE // dil // SPAN):
                tok_rows = pl.ds(dil * SPAN * j + r, SPAN) if dil == 1 else pl.ds(dil * SPAN * j + r, SPAN, stride=dil)
                unit(group_refs[g], j, r * GROUP_WIDTH, tok_rows, g * (GROUP_WIDTH // LANES))

    def combine(i, carry):
        rows = pl.ds(pl.multiple_of(i * COMBINE_ROWS, COMBINE_ROWS), COMBINE_ROWS)
        tiles_per_group = GROUP_WIDTH // LANES
        for t in range(tiles_per_group):
            tiles = [g * tiles_per_group + t for g in range(N_GROUPS)]
            ms = [m_scr[ot, rows, :] for ot in tiles]
            mx = jnp.maximum(jnp.maximum(ms[0], ms[1]), ms[2])
            ws = [jnp.exp2(m - mx) for m in ms]
            den = sum(l_scr[ot, rows, :] * w for ot, w in zip(tiles, ws))
            inv = 1.0 / den
            for ot, w in zip(tiles, ws):
                ow_ref[rows, ot * LANES:(ot + 1) * LANES] = (o_scr[ot, rows, :] * (w * inv)).astype(_BF16)
        return carry

    lax.fori_loop(0, ATTN_TILE // COMBINE_ROWS, combine, 0)


def _mix_mlp_kernel(x_ref, u_ref, uh_ref, ow_ref, g_ref,
                    wdw_ref, bdw_ref, wcn_ref, wco_ref, wao_ref, wo_ref, wnm_ref, w1_ref, w2_ref,
                    out_ref, ext_ref, y_ref, x1_ref, hn_ref, *, tiles_per_seq, n_tiles):
    step = pl.program_id(0)
    tile = jnp.minimum(step, n_tiles - 1)
    first = (tile % tiles_per_seq) == 0
    tm = x_ref.shape[0]
    cur = step % 2
    prev = 1 - cur

    @pl.when(step == 0)
    def _():
        x1_ref[1] = jnp.zeros((tm, D_MODEL), _F32)

    shift = CONV_HALO - (CONV_KERNEL - 1)
    block = CONV_ROWS * CONV_STRIDE
    n_lane_tiles = CONV_WIDTH // LANES
    n_chunks = D_FF // FF_CHUNK
    lane_tiles_per_chunk = n_lane_tiles // n_chunks
    assert lane_tiles_per_chunk * n_chunks == n_lane_tiles

    for c in range(n_lane_tiles):
        cols = slice(c * LANES, (c + 1) * LANES)
        halo = uh_ref[:, cols].astype(_F32)
        ext_ref[c, 0:CONV_HALO, :] = jnp.where(first, jnp.zeros_like(halo), halo)
        ext_ref[c, CONV_HALO:, :] = u_ref[:, cols].astype(_F32)

    def conv_lane_tile(c):
        taps = [jnp.broadcast_to(wdw_ref[c, k:k + 1, :], (CONV_ROWS, LANES)) for k in range(CONV_KERNEL)]
        bias = jnp.broadcast_to(bdw_ref[c], (CONV_ROWS, LANES))

        for base in range(0, tm, block):
            accs = [bias] * CONV_STRIDE
            for s in range(CONV_KERNEL + CONV_STRIDE - 1):
                rows = ext_ref[c, pl.ds(base + shift + s, CONV_ROWS, stride=CONV_STRIDE), :]
                for phase in range(CONV_STRIDE):
                    if 0 <= s - phase < CONV_KERNEL:
                        accs[phase] = accs[phase] + rows * taps[s - phase]
            for phase in range(CONV_STRIDE):
                y_ref[c, pl.ds(base + phase, CONV_ROWS, stride=CONV_STRIDE), :] = accs[phase]

    xp = x1_ref[prev]
    hn_ref[...] = (xp * _rms_scale(xp) * wnm_ref[...]).astype(_BF16)
    out_ref[...] = xp

    def chunk_body(c, carry):
        h = jnp.maximum(_dot(hn_ref[...], w1_ref[c]), 0.0)
        rows = pl.ds(pl.multiple_of(c * FF_CHUNK, FF_CHUNK), FF_CHUNK)
        out_ref[...] += _dot((h * h).astype(_BF16), w2_ref[rows, :])
        for t in range(lane_tiles_per_chunk):
            conv_lane_tile(c * lane_tiles_per_chunk + t)
        return carry

    lax.fori_loop(0, n_chunks, chunk_body, 0)

    branch_b = _dot(ow_ref[...], wao_ref[...])
    y = jnp.concatenate([y_ref[c] for c in range(n_lane_tiles)], axis=1)
    y = y * _rms_scale(y) * wcn_ref[...]
    y = y * _sigmoid(y)
    branch_a = _dot(y.astype(_BF16), wco_ref[...])
    merged = (g_ref[:, 0:D_MODEL].astype(_F32) * branch_a
              + g_ref[:, D_MODEL:2 * D_MODEL].astype(_F32) * branch_b)
    x1_ref[cur] = x_ref[...] + _dot(merged.astype(_BF16), wo_ref[...])


def _const_spec(shape):
    return pl.BlockSpec(shape, lambda *_: (0,) * len(shape))


def _weight_spec(shape):
    return pl.BlockSpec(shape, lambda *_: (0,) * len(shape), pipeline_mode=pl.Buffered(1))


def _rope_constants(tm):
    half = ROPE_DIM // 2
    inv_freq = ROPE_THETA ** (-jnp.arange(0, ROPE_DIM, 2, dtype=_F32) / ROPE_DIM)
    freq = jnp.broadcast_to(inv_freq[:, None], (half, tm))
    d = jnp.arange(LANES) % HEAD_DIM
    rotary = d < ROPE_DIM
    hit = (rotary[None, :] & ((d % half)[None, :] == jnp.arange(half)[:, None])).astype(_F32)
    sign = jnp.where(d < half, -1.0, 1.0)[None, :]
    zero = jnp.zeros_like(hit)
    sel = jnp.concatenate([jnp.concatenate([hit, zero], axis=1),
                           jnp.concatenate([zero, hit * sign], axis=1)], axis=0).astype(_BF16)
    aux = jnp.stack([1.0 - rotary.astype(_F32), (d < half).astype(_F32)]
                    + [jnp.zeros((LANES,), _F32)] * 6)
    return freq, sel, aux


def _params(n_axes):
    return pltpu.CompilerParams(dimension_semantics=("arbitrary",) * n_axes, vmem_limit_bytes=VMEM_LIMIT)


def kernel(x, positions, w_norm_mix, w_in, b_gate, w_dw, b_dw, w_conv_norm, w_conv_out, q_norm_w, k_norm_w,
           w_attn_out, w_o, w_norm_mlp, w_mlp_in, w_mlp_out):
    batch, seq, d_model = x.shape
    depth = w_in.shape[0]
    assert d_model == D_MODEL and seq % ATTN_TILE == 0 and ATTN_TILE % TOKEN_TILE == 0
    n_tok = batch * seq
    tm = TOKEN_TILE
    n_tiles = n_tok // tm
    in_width = w_in.shape[-1]

    xf = x.reshape(n_tok, d_model)
    pos = positions.astype(_F32).reshape(n_tiles, 1, tm)
    freq, sel, aux = _rope_constants(tm)
    head = jnp.arange(GROUP_WIDTH) // HEAD_DIM
    seg = (head[:, None] == head[None, :]).astype(_BF16)

    def row_spec(width):
        return pl.BlockSpec((tm, width), lambda i: (i, 0))

    for layer in range(depth):
        qk_gain = jnp.stack([jnp.tile(q_norm_w[layer], GROUP_WIDTH // HEAD_DIM),
                             jnp.tile(k_norm_w[layer], GROUP_WIDTH // HEAD_DIM)]).astype(_F32)

        def grp_spec(dil):
            return pl.BlockSpec((tm // dil, dil * GROUP_WIDTH), lambda i: (i, 0))

        grp_specs = [grp_spec(dil) for dil in DILATIONS] * 3
        grp_shapes = [jax.ShapeDtypeStruct((n_tok // dil, dil * GROUP_WIDTH), _BF16) for dil in DILATIONS] * 3
        outs = pl.pallas_call(
            _inproj_kernel,
            grid=(n_tiles,),
            in_specs=[row_spec(d_model), pl.BlockSpec((1, 1, tm), lambda i: (i, 0, 0)),
                      _const_spec((1, d_model)),
                      _const_spec((d_model, in_width)), _const_spec((1, 2 * d_model)),
                      _const_spec((2, GROUP_WIDTH)), _const_spec((ROPE_DIM // 2, tm)),
                      _const_spec((ROPE_DIM, 2 * LANES)), _const_spec((8, LANES)),
                      _const_spec((GROUP_WIDTH, GROUP_WIDTH))],
            out_specs=[row_spec(CONV_WIDTH)] + grp_specs + [row_spec(2 * d_model)],
            out_shape=[jax.ShapeDtypeStruct((n_tok, CONV_WIDTH), _BF16)] + grp_shapes
                      + [jax.ShapeDtypeStruct((n_tok, 2 * d_model), _BF16)],
            scratch_shapes=[pltpu.VMEM((2, GROUP_WIDTH // LANES, tm, LANES), _F32)],
            compiler_params=_params(1),
            name="inproj",
        )(xf, pos, w_norm_mix[layer].reshape(1, d_model), w_in[layer].astype(_BF16),
          b_gate[layer].reshape(1, 2 * d_model), qk_gain, freq, sel, aux, seg)
        u, qkv, gates = outs[0], outs[1:10], outs[10]

        n_attn_tiles = seq // ATTN_TILE
        in_specs, operands = [], []
        for g, dil in enumerate(DILATIONS):
            rows, width = ATTN_TILE // dil, dil * GROUP_WIDTH
            spans_per_tile, spans_per_seq = rows // SPAN, seq // dil // SPAN
            cur = pl.BlockSpec((rows, width), lambda b, i: (b * n_attn_tiles + i, 0))
            prev = pl.BlockSpec(
                (SPAN, width),
                lambda b, i, n=spans_per_tile, s=spans_per_seq: (b * s + jnp.maximum(i * n - 1, 0), 0))
            in_specs += [cur, cur, prev, cur, prev]
            operands += [qkv[g], qkv[3 + g], qkv[3 + g], qkv[6 + g], qkv[6 + g]]
        scratch = [pltpu.VMEM((ATTN_WIDTH // LANES, ATTN_TILE, LANES), _F32)] * 3
        ow = pl.pallas_call(
            _attn_kernel,
            grid=(batch, n_attn_tiles),
            in_specs=in_specs,
            out_specs=pl.BlockSpec((ATTN_TILE, ATTN_WIDTH), lambda b, i: (b * n_attn_tiles + i, 0)),
            out_shape=jax.ShapeDtypeStruct((n_tok, ATTN_WIDTH), _BF16),
            scratch_shapes=scratch,
            compiler_params=_params(2),
            name="attention",
        )(*operands)

        n_lane_tiles, n_chunks = CONV_WIDTH // LANES, D_FF // FF_CHUNK

        def tile_spec(width):
            return pl.BlockSpec((tm, width), lambda i: (jnp.minimum(i, n_tiles - 1), 0))

        halo_spec = pl.BlockSpec(
            (CONV_HALO, CONV_WIDTH),
            lambda i: (jnp.maximum(jnp.minimum(i, n_tiles - 1) * (tm // CONV_HALO) - 1, 0), 0))
        xf = pl.pallas_call(
            functools.partial(_mix_mlp_kernel, tiles_per_seq=seq // tm, n_tiles=n_tiles),
            grid=(n_tiles + 1,),
            in_specs=[tile_spec(d_model), tile_spec(CONV_WIDTH), halo_spec, tile_spec(ATTN_WIDTH),
                      tile_spec(2 * d_model),
                      _const_spec((n_lane_tiles, CONV_KERNEL, LANES)), _const_spec((n_lane_tiles, 1, LANES)),
                      _const_spec((1, CONV_WIDTH)), _weight_spec((CONV_WIDTH, d_model)),
                      _weight_spec((ATTN_WIDTH, d_model)), _weight_spec((d_model, d_model)),
                      _const_spec((1, d_model)), _weight_spec((n_chunks, d_model, FF_CHUNK)),
                      _weight_spec((D_FF, d_model))],
            out_specs=pl.BlockSpec((tm, d_model), lambda i: (jnp.maximum(i - 1, 0), 0)),
            out_shape=jax.ShapeDtypeStruct((n_tok, d_model), _F32),
            scratch_shapes=[pltpu.VMEM((CONV_WIDTH // LANES, CONV_HALO + tm, LANES), _F32),
                            pltpu.VMEM((CONV_WIDTH // LANES, tm, LANES), _F32),
                            pltpu.VMEM((2, tm, d_model), _F32),
                            pltpu.VMEM((tm, d_model), _BF16)],
            compiler_params=_params(1),
            name="mixer_mlp",
        )(xf, u, u, ow, gates,
          w_dw[layer].reshape(CONV_KERNEL, n_lane_tiles, LANES).transpose(1, 0, 2),
          b_dw[layer].reshape(n_lane_tiles, 1, LANES),
          w_conv_norm[layer].reshape(1, CONV_WIDTH), w_conv_out[layer].astype(_BF16),
          w_attn_out[layer].astype(_BF16), w_o[layer].astype(_BF16),
          w_norm_mlp[layer].reshape(1, d_model),
          w_mlp_in[layer].astype(_BF16).reshape(d_model, n_chunks, FF_CHUNK).transpose(1, 0, 2),
          w_mlp_out[layer].astype(_BF16))

    return xf.reshape(batch, seq, d_model)
```

```python
import functools

import jax
import jax.numpy as jnp
from jax import lax
from jax.experimental import pallas as pl
from jax.experimental.pallas import tpu as pltpu

D_MODEL = 1024
HEAD_DIM = 64
HEADS_PER_GROUP = 4
GROUP_WIDTH = HEADS_PER_GROUP * HEAD_DIM
DILATIONS = (1, 4, 16)
SPAN = 128
N_GROUPS = len(DILATIONS)
ATTN_WIDTH = N_GROUPS * GROUP_WIDTH
ROPE_THETA = 500000.0
ROPE_DIM = HEAD_DIM // 4
CONV_WIDTH = D_MODEL
CONV_KERNEL = 31
D_FF = 4 * D_MODEL
NORM_EPS = 1e-6
MASK_VALUE = -1e30
LOG2_E = 1.4426950408889634

LANES = 128
CONV_HALO = 32
ATTN_TILE = SPAN * DILATIONS[-1]
TOKEN_TILE = 512
INPROJ_TILE = 1024
CONV_ROWS = 32
CONV_STRIDE = 4
FF_CHUNK = 2048
COMBINE_ROWS = 256
VMEM_LIMIT = 56 * 1024 * 1024

_F32 = jnp.float32
_BF16 = jnp.bfloat16


def _dot(a, b):
    return jnp.dot(a, b, preferred_element_type=_F32)


def _sigmoid(z):
    return 0.5 * jnp.tanh(0.5 * z) + 0.5


def _rms_scale(xf):
    return lax.rsqrt(jnp.mean(xf * xf, axis=-1, keepdims=True) + NORM_EPS)


def _inproj_kernel(x_ref, pos_ref, wn_ref, w_ref, bg_ref, qkw_ref, freq_ref, sel_ref, aux_ref, seg_ref,
                   u_ref, q1_ref, q2_ref, q3_ref, k1_ref, k2_ref, k3_ref, v1_ref, v2_ref, v3_ref, g_ref,
                   z_ref):
    tm = x_ref.shape[0]
    x = x_ref[...]
    n = (x * wn_ref[...]).astype(_BF16)
    r = _rms_scale(x)

    ang = freq_ref[...] * pos_ref[0]
    cs = jnp.concatenate([jnp.cos(ang), jnp.sin(ang)], axis=0).T
    cs_hi = cs.astype(_BF16)
    cs_lo = (cs - cs_hi.astype(_F32)).astype(_BF16)
    tables = _dot(cs_hi, sel_ref[...]) + _dot(cs_lo, sel_ref[...])
    cos = tables[:, 0:LANES] + aux_ref[0:1, :]
    sin = tables[:, LANES:2 * LANES]
    first_half = aux_ref[1:2, :] > 0.5
    seg = seg_ref[...]

    def store_by_residue(slot, out_ref, dil):
        for t in range(GROUP_WIDTH // LANES):
            for r in range(dil):
                rows = z_ref[slot, t] if dil == 1 else z_ref[slot, t, pl.ds(r, tm // dil, stride=dil), :]
                c = r * GROUP_WIDTH + t * LANES
                out_ref[:, c:c + LANES] = rows.astype(_BF16)

    def finish_qk(z, slot, gain_row, scale, out_ref, dil):
        z = z * r
        ss = _dot((z * z).astype(_BF16), seg)
        z = z * lax.rsqrt(ss * (1.0 / HEAD_DIM) + NORM_EPS) * qkw_ref[gain_row:gain_row + 1, :]
        for t in range(GROUP_WIDTH // LANES):
            zt = z[:, t * LANES:(t + 1) * LANES]
            partner = jnp.where(first_half, pltpu.roll(zt, LANES - ROPE_DIM // 2, 1),
                                pltpu.roll(zt, ROPE_DIM // 2, 1))
            z_ref[slot, t] = (zt * cos + partner * sin) * scale
        store_by_residue(slot, out_ref, dil)

    def finish_v(z, slot, out_ref, dil):
        for t in range(GROUP_WIDTH // LANES):
            z_ref[slot, t] = z[:, t * LANES:(t + 1) * LANES] * r
        store_by_residue(slot, out_ref, dil)

    def finish_gate(z, t):
        z = z * r + bg_ref[:, t * D_MODEL:(t + 1) * D_MODEL]
        g_ref[:, t * D_MODEL:(t + 1) * D_MODEL] = _sigmoid(z).astype(_BF16)

    def finish_glu(z):
        z = z * r
        u_ref[...] = (z[:, 0:CONV_WIDTH] * _sigmoid(z[:, CONV_WIDTH:2 * CONV_WIDTH])).astype(_BF16)

    c0 = 2 * CONV_WIDTH
    c3 = c0 + 3 * ATTN_WIDTH
    q_refs = (q1_ref, q2_ref, q3_ref)
    k_refs = (k1_ref, k2_ref, k3_ref)
    v_refs = (v1_ref, v2_ref, v3_ref)

    segments = [(c3, D_MODEL, functools.partial(finish_gate, t=0)),
                (c3 + D_MODEL, D_MODEL, functools.partial(finish_gate, t=1))]
    for g, dil in enumerate(DILATIONS):
        segments.append((c0 + g * GROUP_WIDTH, GROUP_WIDTH,
                         functools.partial(finish_qk, slot=0, gain_row=0, scale=HEAD_DIM ** -0.5 * LOG2_E,
                                           out_ref=q_refs[g], dil=dil)))
        segments.append((c0 + ATTN_WIDTH + g * GROUP_WIDTH, GROUP_WIDTH,
                         functools.partial(finish_qk, slot=1, gain_row=1, scale=1.0,
                                           out_ref=k_refs[g], dil=dil)))
    segments.append((0, 2 * CONV_WIDTH, finish_glu))
    for g, dil in enumerate(DILATIONS):
        segments.append((c0 + 2 * ATTN_WIDTH + g * GROUP_WIDTH, GROUP_WIDTH,
                         functools.partial(finish_v, slot=g % 2, out_ref=v_refs[g], dil=dil)))

    pending = None
    for col, width, finish in segments:
        z = _dot(n, w_ref[:, col:col + width])
        if pending is not None:
            pending()
        pending = functools.partial(finish, z)
    pending()


def _attn_kernel(q1_ref, k1_ref, k1p_ref, v1_ref, v1p_ref,
                 q2_ref, k2_ref, k2p_ref, v2_ref, v2p_ref,
                 q3_ref, k3_ref, k3p_ref, v3_ref, v3p_ref,
                 ow_ref,
                 o_scr, l_scr, m_scr):
    tile = pl.program_id(1)

    qi = lax.broadcasted_iota(jnp.int32, (SPAN, 2 * SPAN), 0)
    ki = lax.broadcasted_iota(jnp.int32, (SPAN, 2 * SPAN), 1)
    dist = qi + SPAN - ki
    band = jnp.where((dist >= 0) & (dist <= SPAN), 0.0, MASK_VALUE).astype(_F32)
    band_first = jnp.where((ki >= SPAN) | (tile > 0), band, MASK_VALUE)
    band2 = jnp.concatenate([band, band], axis=0)
    band_first2 = jnp.concatenate([band_first, band_first], axis=0)

    lane = lax.broadcasted_iota(jnp.int32, (SPAN, LANES), 1)
    low_head = lane < HEAD_DIM
    ones = jnp.ones((2 * SPAN, LANES), _BF16)

    def with_prev(cur_ref, prev_ref, j, cols):
        if j == 0:
            return jnp.concatenate([prev_ref[:, cols], cur_ref[0:SPAN, cols]], axis=0)
        return cur_ref[(j - 1) * SPAN:(j + 1) * SPAN, cols]

    def unit(refs, j, col0, tok_rows, out_tile0):
        q_ref, k_ref, kp_ref, v_ref, vp_ref = refs
        bias2 = band_first2 if j == 0 else band2
        for pair in range(GROUP_WIDTH // LANES):
            cols = slice(col0 + pair * LANES, col0 + (pair + 1) * LANES)
            q = q_ref[j * SPAN:(j + 1) * SPAN, cols]
            kk = with_prev(k_ref, kp_ref, j, cols)
            vv = with_prev(v_ref, vp_ref, j, cols)
            zero = jnp.zeros_like(q)
            q2 = jnp.concatenate([jnp.where(low_head, q, zero), jnp.where(low_head, zero, q)], axis=0)
            s = lax.dot_general(q2, kk, (((1,), (1,)), ((), ())), preferred_element_type=_F32) + bias2
            m = jnp.max(s, axis=-1, keepdims=True)
            p = jnp.exp2(s - m).astype(_BF16)
            o2 = _dot(p, jnp.concatenate([vv, ones], axis=1))
            m2 = jnp.broadcast_to(m, (2 * SPAN, LANES))
            ot = out_tile0 + pair
            o_scr[ot, tok_rows, :] = jnp.where(low_head, o2[0:SPAN, 0:LANES], o2[SPAN:2 * SPAN, 0:LANES])
            l_scr[ot, tok_rows, :] = jnp.where(low_head, o2[0:SPAN, LANES:2 * LANES],
                                               o2[SPAN:2 * SPAN, LANES:2 * LANES])
            m_scr[ot, tok_rows, :] = jnp.where(low_head, m2[0:SPAN], m2[SPAN:2 * SPAN])

    group_refs = ((q1_ref, k1_ref, k1p_ref, v1_ref, v1p_ref),
                  (q2_ref, k2_ref, k2p_ref, v2_ref, v2p_ref),
                  (q3_ref, k3_ref, k3p_ref, v3_ref, v3p_ref))
    for g, dil in enumerate(DILATIONS):
        for r in range(dil):
            for j in range(ATTN_TILE // dil // SPAN):
                tok_rows = pl.ds(dil * SPAN * j + r, SPAN) if dil == 1 else pl.ds(dil * SPAN * j + r, SPAN, stride=dil)
                unit(group_refs[g], j, r * GROUP_WIDTH, tok_rows, g * (GROUP_WIDTH // LANES))

    def combine(i, carry):
        rows = pl.ds(pl.multiple_of(i * COMBINE_ROWS, COMBINE_ROWS), COMBINE_ROWS)
        tiles_per_group = GROUP_WIDTH // LANES
        for t in range(tiles_per_group):
            tiles = [g * tiles_per_group + t for g in range(N_GROUPS)]
            ms = [m_scr[ot, rows, :] for ot in tiles]
            mx = jnp.maximum(jnp.maximum(ms[0], ms[1]), ms[2])
            ws = [jnp.exp2(m - mx) for m in ms]
            den = sum(l_scr[ot, rows, :] * w for ot, w in zip(tiles, ws))
            inv = 1.0 / den
            for ot, w in zip(tiles, ws):
                ow_ref[rows, ot * LANES:(ot + 1) * LANES] = (o_scr[ot, rows, :] * (w * inv)).astype(_BF16)
        return carry

    lax.fori_loop(0, ATTN_TILE // COMBINE_ROWS, combine, 0)


def _mix_mlp_kernel(x_ref, u_ref, uh_ref, ow_ref, g_ref,
                    wdw_ref, bdw_ref, wcn_ref, wco_ref, wao_ref, wo_ref, wnm_ref, w1_ref, w2_ref,
                    out_ref, ext_ref, y_ref, x1_ref, hn_ref, *, tiles_per_seq, n_tiles):
    step = pl.program_id(0)
    tile = jnp.minimum(step, n_tiles - 1)
    first = (tile % tiles_per_seq) == 0
    tm = x_ref.shape[0]
    cur = step % 2
    prev = 1 - cur

    @pl.when(step == 0)
    def _():
        x1_ref[1] = jnp.zeros((tm, D_MODEL), _F32)

    shift = CONV_HALO - (CONV_KERNEL - 1)
    block = CONV_ROWS * CONV_STRIDE
    n_lane_tiles = CONV_WIDTH // LANES
    n_chunks = D_FF // FF_CHUNK
    lane_tiles_per_chunk = n_lane_tiles // n_chunks
    assert lane_tiles_per_chunk * n_chunks == n_lane_tiles

    for c in range(n_lane_tiles):
        cols = slice(c * LANES, (c + 1) * LANES)
        halo = uh_ref[:, cols].astype(_F32)
        ext_ref[c, 0:CONV_HALO, :] = jnp.where(first, jnp.zeros_like(halo), halo)
        ext_ref[c, CONV_HALO:, :] = u_ref[:, cols].astype(_F32)

    def conv_lane_tile(c):
        taps = [jnp.broadcast_to(wdw_ref[c, k:k + 1, :], (CONV_ROWS, LANES)) for k in range(CONV_KERNEL)]
        bias = jnp.broadcast_to(bdw_ref[c], (CONV_ROWS, LANES))

        for base in range(0, tm, block):
            accs = [bias] * CONV_STRIDE
            for s in range(CONV_KERNEL + CONV_STRIDE - 1):
                rows = ext_ref[c, pl.ds(base + shift + s, CONV_ROWS, stride=CONV_STRIDE), :]
                for phase in range(CONV_STRIDE):
                    if 0 <= s - phase < CONV_KERNEL:
                        accs[phase] = accs[phase] + rows * taps[s - phase]
            for phase in range(CONV_STRIDE):
                y_ref[c, pl.ds(base + phase, CONV_ROWS, stride=CONV_STRIDE), :] = accs[phase]

    xp = x1_ref[prev]
    hn_ref[...] = (xp * _rms_scale(xp) * wnm_ref[...]).astype(_BF16)
    out_ref[...] = xp

    def chunk_body(c, carry):
        h = jnp.maximum(_dot(hn_ref[...], w1_ref[c]), 0.0)
        rows = pl.ds(pl.multiple_of(c * FF_CHUNK, FF_CHUNK), FF_CHUNK)
        out_ref[...] += _dot((h * h).astype(_BF16), w2_ref[rows, :])
        for t in range(lane_tiles_per_chunk):
            conv_lane_tile(c * lane_tiles_per_chunk + t)
        return carry

    lax.fori_loop(0, n_chunks, chunk_body, 0)

    branch_b = _dot(ow_ref[...], wao_ref[...])
    y = jnp.concatenate([y_ref[c] for c in range(n_lane_tiles)], axis=1)
    y = y * _rms_scale(y) * wcn_ref[...]
    y = y * _sigmoid(y)
    branch_a = _dot(y.astype(_BF16), wco_ref[...])
    merged = (g_ref[:, 0:D_MODEL].astype(_F32) * branch_a
              + g_ref[:, D_MODEL:2 * D_MODEL].astype(_F32) * branch_b)
    x1_ref[cur] = x_ref[...] + _dot(merged.astype(_BF16), wo_ref[...])


def _const_spec(shape):
    return pl.BlockSpec(shape, lambda *_: (0,) * len(shape))


def _weight_spec(shape):
    return pl.BlockSpec(shape, lambda *_: (0,) * len(shape), pipeline_mode=pl.Buffered(1))


def _rope_constants(tm):
    half = ROPE_DIM // 2
    inv_freq = ROPE_THETA ** (-jnp.arange(0, ROPE_DIM, 2, dtype=_F32) / ROPE_DIM)
    freq = jnp.broadcast_to(inv_freq[:, None], (half, tm))
    d = jnp.arange(LANES) % HEAD_DIM
    rotary = d < ROPE_DIM
    hit = (rotary[None, :] & ((d % half)[None, :] == jnp.arange(half)[:, None])).astype(_F32)
    sign = jnp.where(d < half, -1.0, 1.0)[None, :]
    zero = jnp.zeros_like(hit)
    sel = jnp.concatenate([jnp.concatenate([hit, zero], axis=1),
                           jnp.concatenate([zero, hit * sign], axis=1)], axis=0).astype(_BF16)
    aux = jnp.stack([1.0 - rotary.astype(_F32), (d < half).astype(_F32)]
                    + [jnp.zeros((LANES,), _F32)] * 6)
    return freq, sel, aux


def _params(n_axes):
    return pltpu.CompilerParams(dimension_semantics=("arbitrary",) * n_axes, vmem_limit_bytes=VMEM_LIMIT)


def kernel(x, positions, w_norm_mix, w_in, b_gate, w_dw, b_dw, w_conv_norm, w_conv_out, q_norm_w, k_norm_w,
           w_attn_out, w_o, w_norm_mlp, w_mlp_in, w_mlp_out):
    batch, seq, d_model = x.shape
    depth = w_in.shape[0]
    assert d_model == D_MODEL and seq % ATTN_TILE == 0 and ATTN_TILE % TOKEN_TILE == 0
    n_tok = batch * seq
    tm = TOKEN_TILE
    n_tiles = n_tok // tm
    in_width = w_in.shape[-1]

    tm_in = INPROJ_TILE
    xf = x.reshape(n_tok, d_model)
    pos = positions.astype(_F32).reshape(n_tok // tm_in, 1, tm_in)
    freq, sel, aux = _rope_constants(tm_in)
    head = jnp.arange(GROUP_WIDTH) // HEAD_DIM
    seg = (head[:, None] == head[None, :]).astype(_BF16)

    def row_spec(width):
        return pl.BlockSpec((tm_in, width), lambda i: (i, 0))

    for layer in range(depth):
        qk_gain = jnp.stack([jnp.tile(q_norm_w[layer], GROUP_WIDTH // HEAD_DIM),
                             jnp.tile(k_norm_w[layer], GROUP_WIDTH // HEAD_DIM)]).astype(_F32)

        def grp_spec(dil):
            return pl.BlockSpec((tm_in // dil, dil * GROUP_WIDTH), lambda i: (i, 0))

        grp_specs = [grp_spec(dil) for dil in DILATIONS] * 3
        grp_shapes = [jax.ShapeDtypeStruct((n_tok // dil, dil * GROUP_WIDTH), _BF16) for dil in DILATIONS] * 3
        outs = pl.pallas_call(
            _inproj_kernel,
            grid=(n_tok // tm_in,),
            in_specs=[row_spec(d_model), pl.BlockSpec((1, 1, tm_in), lambda i: (i, 0, 0)),
                      _const_spec((1, d_model)),
                      _const_spec((d_model, in_width)), _const_spec((1, 2 * d_model)),
                      _const_spec((2, GROUP_WIDTH)), _const_spec((ROPE_DIM // 2, tm_in)),
                      _const_spec((ROPE_DIM, 2 * LANES)), _const_spec((8, LANES)),
                      _const_spec((GROUP_WIDTH, GROUP_WIDTH))],
            out_specs=[row_spec(CONV_WIDTH)] + grp_specs + [row_spec(2 * d_model)],
            out_shape=[jax.ShapeDtypeStruct((n_tok, CONV_WIDTH), _BF16)] + grp_shapes
                      + [jax.ShapeDtypeStruct((n_tok, 2 * d_model), _BF16)],
            scratch_shapes=[pltpu.VMEM((2, GROUP_WIDTH // LANES, tm_in, LANES), _F32)],
            compiler_params=_params(1),
            name="inproj",
        )(xf, pos, w_norm_mix[layer].reshape(1, d_model), w_in[layer].astype(_BF16),
          b_gate[layer].reshape(1, 2 * d_model), qk_gain, freq, sel, aux, seg)
        u, qkv, gates = outs[0], outs[1:10], outs[10]

        n_attn_tiles = seq // ATTN_TILE
        in_specs, operands = [], []
        for g, dil in enumerate(DILATIONS):
            rows, width = ATTN_TILE // dil, dil * GROUP_WIDTH
            spans_per_tile, spans_per_seq = rows // SPAN, seq // dil // SPAN
            cur = pl.BlockSpec((rows, width), lambda b, i: (b * n_attn_tiles + i, 0))
            prev = pl.BlockSpec(
                (SPAN, width),
                lambda b, i, n=spans_per_tile, s=spans_per_seq: (b * s + jnp.maximum(i * n - 1, 0), 0))
            in_specs += [cur, cur, prev, cur, prev]
            operands += [qkv[g], qkv[3 + g], qkv[3 + g], qkv[6 + g], qkv[6 + g]]
        scratch = [pltpu.VMEM((ATTN_WIDTH // LANES, ATTN_TILE, LANES), _F32)] * 3
        ow = pl.pallas_call(
            _attn_kernel,
            grid=(batch, n_attn_tiles),
            in_specs=in_specs,
            out_specs=pl.BlockSpec((ATTN_TILE, ATTN_WIDTH), lambda b, i: (b * n_attn_tiles + i, 0)),
            out_shape=jax.ShapeDtypeStruct((n_tok, ATTN_WIDTH), _BF16),
            scratch_shapes=scratch,
            compiler_params=_params(2),
            name="attention",
        )(*operands)

        n_lane_tiles, n_chunks = CONV_WIDTH // LANES, D_FF // FF_CHUNK

        def tile_spec(width):
            return pl.BlockSpec((tm, width), lambda i: (jnp.minimum(i, n_tiles - 1), 0))

        halo_spec = pl.BlockSpec(
            (CONV_HALO, CONV_WIDTH),
            lambda i: (jnp.maximum(jnp.minimum(i, n_tiles - 1) * (tm // CONV_HALO) - 1, 0), 0))
        xf = pl.pallas_call(
            functools.partial(_mix_mlp_kernel, tiles_per_seq=seq // tm, n_tiles=n_tiles),
            grid=(n_tiles + 1,),
            in_specs=[tile_spec(d_model), tile_spec(CONV_WIDTH), halo_spec, tile_spec(ATTN_WIDTH),
                      tile_spec(2 * d_model),
                      _const_spec((n_lane_tiles, CONV_KERNEL, LANES)), _const_spec((n_lane_tiles, 1, LANES)),
                      _const_spec((1, CONV_WIDTH)), _weight_spec((CONV_WIDTH, d_model)),
                      _weight_spec((ATTN_WIDTH, d_model)), _weight_spec((d_model, d_model)),
                      _const_spec((1, d_model)), _weight_spec((n_chunks, d_model, FF_CHUNK)),
                      _weight_spec((D_FF, d_model))],
            out_specs=pl.BlockSpec((tm, d_model), lambda i: (jnp.maximum(i - 1, 0), 0)),
            out_shape=jax.ShapeDtypeStruct((n_tok, d_model), _F32),
            scratch_shapes=[pltpu.VMEM((CONV_WIDTH // LANES, CONV_HALO + tm, LANES), _F32),
                            pltpu.VMEM((CONV_WIDTH // LANES, tm, LANES), _F32),
                            pltpu.VMEM((2, tm, d_model), _F32),
                            pltpu.VMEM((tm, d_model), _BF16)],
            compiler_params=_params(1),
            name="mixer_mlp",
        )(xf, u, u, ow, gates,
          w_dw[layer].reshape(CONV_KERNEL, n_lane_tiles, LANES).transpose(1, 0, 2),
          b_dw[layer].reshape(n_lane_tiles, 1, LANES),
          w_conv_norm[layer].reshape(1, CONV_WIDTH), w_conv_out[layer].astype(_BF16),
          w_attn_out[layer].astype(_BF16), w_o[layer].astype(_BF16),
          w_norm_mlp[layer].reshape(1, d_model),
          w_mlp_in[layer].astype(_BF16).reshape(d_model, n_chunks, FF_CHUNK).transpose(1, 0, 2),
          w_mlp_out[layer].astype(_BF16))

    return xf.reshape(batch, seq, d_model)
```

```python
import functools

import jax
import jax.numpy as jnp
from jax import lax
from jax.experimental import pallas as pl
from jax.experimental.pallas import tpu as pltpu

D_MODEL = 1024
HEAD_DIM = 64
HEADS_PER_GROUP = 4
GROUP_WIDTH = HEADS_PER_GROUP * HEAD_DIM
DILATIONS = (1, 4, 16)
SPAN = 128
N_GROUPS = len(DILATIONS)
ATTN_WIDTH = N_GROUPS * GROUP_WIDTH
ROPE_THETA = 500000.0
ROPE_DIM = HEAD_DIM // 4
CONV_WIDTH = D_MODEL
CONV_KERNEL = 31
D_FF = 4 * D_MODEL
NORM_EPS = 1e-6
MASK_VALUE = -1e30
LOG2_E = 1.4426950408889634

LANES = 128
CONV_HALO = 32
ATTN_TILE = SPAN * DILATIONS[-1]
TOKEN_TILE = 512
INPROJ_TILE = 1024
CONV_ROWS = 32
CONV_STRIDE = 4
FF_CHUNK = 2048
COMBINE_ROWS = 256
VMEM_LIMIT = 56 * 1024 * 1024

_F32 = jnp.float32
_BF16 = jnp.bfloat16


def _dot(a, b):
    return jnp.dot(a, b, preferred_element_type=_F32)


def _sigmoid(z):
    return 0.5 * jnp.tanh(0.5 * z) + 0.5


def _rms_scale(xf):
    return lax.rsqrt(jnp.mean(xf * xf, axis=-1, keepdims=True) + NORM_EPS)


def _inproj_kernel(x_ref, pos_ref, wn_ref, w_ref, bg_ref, qkw_ref, freq_ref, sel_ref, aux_ref, seg_ref,
                   u_ref, q1_ref, q2_ref, q3_ref, k1_ref, k2_ref, k3_ref, v1_ref, v2_ref, v3_ref, g_ref,
                   z_ref):
    tm = x_ref.shape[0]
    x = x_ref[...]
    n = (x * wn_ref[...]).astype(_BF16)
    r = _rms_scale(x)

    ang = freq_ref[...] * pos_ref[0]
    cs = jnp.concatenate([jnp.cos(ang), jnp.sin(ang)], axis=0).T
    cs_hi = cs.astype(_BF16)
    cs_lo = (cs - cs_hi.astype(_F32)).astype(_BF16)
    tables = _dot(cs_hi, sel_ref[...]) + _dot(cs_lo, sel_ref[...])
    cos = tables[:, 0:LANES] + aux_ref[0:1, :]
    sin = tables[:, LANES:2 * LANES]
    first_half = aux_ref[1:2, :] > 0.5
    seg = seg_ref[...]

    def store_by_residue(slot, out_ref, dil):
        for t in range(GROUP_WIDTH // LANES):
            for r in range(dil):
                rows = z_ref[slot, t] if dil == 1 else z_ref[slot, t, pl.ds(r, tm // dil, stride=dil), :]
                c = r * GROUP_WIDTH + t * LANES
                out_ref[:, c:c + LANES] = rows.astype(_BF16)

    def finish_qk(z, slot, gain_row, scale, out_ref, dil):
        z = z * r
        ss = _dot((z * z).astype(_BF16), seg)
        z = z * lax.rsqrt(ss * (1.0 / HEAD_DIM) + NORM_EPS) * qkw_ref[gain_row:gain_row + 1, :]
        for t in range(GROUP_WIDTH // LANES):
            zt = z[:, t * LANES:(t + 1) * LANES]
            partner = jnp.where(first_half, pltpu.roll(zt, LANES - ROPE_DIM // 2, 1),
                                pltpu.roll(zt, ROPE_DIM // 2, 1))
            z_ref[slot, t] = (zt * cos + partner * sin) * scale
        store_by_residue(slot, out_ref, dil)

    def finish_v(z, slot, out_ref, dil):
        for t in range(GROUP_WIDTH // LANES):
            z_ref[slot, t] = z[:, t * LANES:(t + 1) * LANES] * r
        store_by_residue(slot, out_ref, dil)

    def finish_gate(z, t):
        z = z * r + bg_ref[:, t * D_MODEL:(t + 1) * D_MODEL]
        g_ref[:, t * D_MODEL:(t + 1) * D_MODEL] = _sigmoid(z).astype(_BF16)

    def finish_glu(z):
        z = z * r
        u_ref[...] = (z[:, 0:CONV_WIDTH] * _sigmoid(z[:, CONV_WIDTH:2 * CONV_WIDTH])).astype(_BF16)

    c0 = 2 * CONV_WIDTH
    c3 = c0 + 3 * ATTN_WIDTH
    q_refs = (q1_ref, q2_ref, q3_ref)
    k_refs = (k1_ref, k2_ref, k3_ref)
    v_refs = (v1_ref, v2_ref, v3_ref)

    segments = [(c3, D_MODEL, functools.partial(finish_gate, t=0)),
                (c3 + D_MODEL, D_MODEL, functools.partial(finish_gate, t=1))]
    for g, dil in enumerate(DILATIONS):
        segments.append((c0 + g * GROUP_WIDTH, GROUP_WIDTH,
                         functools.partial(finish_qk, slot=0, gain_row=0, scale=HEAD_DIM ** -0.5 * LOG2_E,
                                           out_ref=q_refs[g], dil=dil)))
        segments.append((c0 + ATTN_WIDTH + g * GROUP_WIDTH, GROUP_WIDTH,
                         functools.partial(finish_qk, slot=1, gain_row=1, scale=1.0,
                                           out_ref=k_refs[g], dil=dil)))
    segments.append((0, 2 * CONV_WIDTH, finish_glu))
    for g, dil in enumerate(DILATIONS):
        segments.append((c0 + 2 * ATTN_WIDTH + g * GROUP_WIDTH, GROUP_WIDTH,
                         functools.partial(finish_v, slot=g % 2, out_ref=v_refs[g], dil=dil)))

    pending = None
    for col, width, finish in segments:
        z = _dot(n, w_ref[:, col:col + width])
        if pending is not None:
            pending()
        pending = functools.partial(finish, z)
    pending()


def _attn_kernel(q1_ref, k1_ref, k1p_ref, v1_ref, v1p_ref,
                 q2_ref, k2_ref, k2p_ref, v2_ref, v2p_ref,
                 q3_ref, k3_ref, k3p_ref, v3_ref, v3p_ref,
                 ow_ref,
                 o_scr, l_scr, m_scr):
    tile = pl.program_id(1)

    qi = lax.broadcasted_iota(jnp.int32, (SPAN, 2 * SPAN), 0)
    ki = lax.broadcasted_iota(jnp.int32, (SPAN, 2 * SPAN), 1)
    dist = qi + SPAN - ki
    band = jnp.where((dist >= 0) & (dist <= SPAN), 0.0, MASK_VALUE).astype(_F32)
    band_first = jnp.where((ki >= SPAN) | (tile > 0), band, MASK_VALUE)
    band2 = jnp.concatenate([band, band], axis=0)
    band_first2 = jnp.concatenate([band_first, band_first], axis=0)

    lane = lax.broadcasted_iota(jnp.int32, (SPAN, LANES), 1)
    low_head = lane < HEAD_DIM
    ones = jnp.ones((2 * SPAN, LANES), _BF16)

    def with_prev(cur_ref, prev_ref, j, cols):
        if j == 0:
            return jnp.concatenate([prev_ref[:, cols], cur_ref[0:SPAN, cols]], axis=0)
        return cur_ref[(j - 1) * SPAN:(j + 1) * SPAN, cols]

    def unit(refs, j, col0, tok_rows, out_tile0):
        q_ref, k_ref, kp_ref, v_ref, vp_ref = refs
        bias2 = band_first2 if j == 0 else band2
        for pair in range(GROUP_WIDTH // LANES):
            cols = slice(col0 + pair * LANES, col0 + (pair + 1) * LANES)
            q = q_ref[j * SPAN:(j + 1) * SPAN, cols]
            kk = with_prev(k_ref, kp_ref, j, cols)
            vv = with_prev(v_ref, vp_ref, j, cols)
            zero = jnp.zeros_like(q)
            q2 = jnp.concatenate([jnp.where(low_head, q, zero), jnp.where(low_head, zero, q)], axis=0)
            s = lax.dot_general(q2, kk, (((1,), (1,)), ((), ())), preferred_element_type=_F32) + bias2
            m = jnp.max(s, axis=-1, keepdims=True)
            p = jnp.exp2(s - m).astype(_BF16)
            o2 = _dot(p, jnp.concatenate([vv, ones], axis=1))
            m2 = jnp.broadcast_to(m, (2 * SPAN, LANES))
            ot = out_tile0 + pair
            o_scr[ot, tok_rows, :] = jnp.where(low_head, o2[0:SPAN, 0:LANES], o2[SPAN:2 * SPAN, 0:LANES])
            l_scr[ot, tok_rows, :] = jnp.where(low_head, o2[0:SPAN, LANES:2 * LANES],
                                               o2[SPAN:2 * SPAN, LANES:2 * LANES])
            m_scr[ot, tok_rows, :] = jnp.where(low_head, m2[0:SPAN], m2[SPAN:2 * SPAN])

    group_refs = ((q1_ref, k1_ref, k1p_ref, v1_ref, v1p_ref),
                  (q2_ref, k2_ref, k2p_ref, v2_ref, v2p_ref),
                  (q3_ref, k3_ref, k3p_ref, v3_ref, v3p_ref))
    for g, dil in enumerate(DILATIONS):
        for r in range(dil):
            for j in range(ATTN_TILE // dil // SPAN):
                tok_rows = pl.ds(dil * SPAN * j + r, SPAN) if dil == 1 else pl.ds(dil * SPAN * j + r, SPAN, stride=dil)
                unit(group_refs[g], j, r * GROUP_WIDTH, tok_rows, g * (GROUP_WIDTH // LANES))

    def combine(i, carry):
        rows = pl.ds(pl.multiple_of(i * COMBINE_ROWS, COMBINE_ROWS), COMBINE_ROWS)
        tiles_per_group = GROUP_WIDTH // LANES
        for t in range(tiles_per_group):
            tiles = [g * tiles_per_group + t for g in range(N_GROUPS)]
            ms = [m_scr[ot, rows, :] for ot in tiles]
            mx = jnp.maximum(jnp.maximum(ms[0], ms[1]), ms[2])
            ws = [jnp.exp2(m - mx) for m in ms]
            den = sum(l_scr[ot, rows, :] * w for ot, w in zip(tiles, ws))
            inv = 1.0 / den
            for ot, w in zip(tiles, ws):
                ow_ref[rows, ot * LANES:(ot + 1) * LANES] = (o_scr[ot, rows, :] * (w * inv)).astype(_BF16)
        return carry

    lax.fori_loop(0, ATTN_TILE // COMBINE_ROWS, combine, 0)


def _mix_mlp_kernel(x_ref, u_ref, uh_ref, ow_ref, g_ref,
                    wdw_ref, bdw_ref, wcn_ref, wco_ref, wao_ref, wo_ref, wnm_ref, w1_ref, w2_ref,
                    out_ref, ext_ref, y_ref, x1_ref, hn_ref, *, tiles_per_seq, n_tiles):
    step = pl.program_id(0)
    tile = jnp.minimum(step, n_tiles - 1)
    first = (tile % tiles_per_seq) == 0
    tm = x_ref.shape[0]
    cur = step % 2
    prev = 1 - cur

    @pl.when(step == 0)
    def _():
        x1_ref[1] = jnp.zeros((tm, D_MODEL), _F32)

    shift = CONV_HALO - (CONV_KERNEL - 1)
    block = CONV_ROWS * CONV_STRIDE
    n_lane_tiles = CONV_WIDTH // LANES
    n_chunks = D_FF // FF_CHUNK
    lane_tiles_per_chunk = n_lane_tiles // n_chunks
    assert lane_tiles_per_chunk * n_chunks == n_lane_tiles

    for c in range(n_lane_tiles):
        cols = slice(c * LANES, (c + 1) * LANES)
        halo = uh_ref[:, cols].astype(_F32)
        ext_ref[c, 0:CONV_HALO, :] = jnp.where(first, jnp.zeros_like(halo), halo)
        ext_ref[c, CONV_HALO:, :] = u_ref[:, cols].astype(_F32)

    def conv_lane_tile(c):
        taps = [jnp.broadcast_to(wdw_ref[c, k:k + 1, :], (CONV_ROWS, LANES)) for k in range(CONV_KERNEL)]
        bias = jnp.broadcast_to(bdw_ref[c], (CONV_ROWS, LANES))

        for base in range(0, tm, block):
            accs = [bias] * CONV_STRIDE
            for s in range(CONV_KERNEL + CONV_STRIDE - 1):
                rows = ext_ref[c, pl.ds(base + shift + s, CONV_ROWS, stride=CONV_STRIDE), :]
                for phase in range(CONV_STRIDE):
                    if 0 <= s - phase < CONV_KERNEL:
                        accs[phase] = accs[phase] + rows * taps[s - phase]
            for phase in range(CONV_STRIDE):
                y_ref[c, pl.ds(base + phase, CONV_ROWS, stride=CONV_STRIDE), :] = accs[phase]

    xp = x1_ref[prev]
    hn_ref[...] = (xp * _rms_scale(xp) * wnm_ref[...]).astype(_BF16)
    out_ref[...] = xp

    def chunk_body(c, carry):
        h = jnp.maximum(_dot(hn_ref[...], w1_ref[c]), 0.0)
        rows = pl.ds(pl.multiple_of(c * FF_CHUNK, FF_CHUNK), FF_CHUNK)
        out_ref[...] += _dot((h * h).astype(_BF16), w2_ref[rows, :])
        for t in range(lane_tiles_per_chunk):
            conv_lane_tile(c * lane_tiles_per_chunk + t)
        return carry

    lax.fori_loop(0, n_chunks, chunk_body, 0)

    branch_b = _dot(ow_ref[...], wao_ref[...])
    y = jnp.concatenate([y_ref[c] for c in range(n_lane_tiles)], axis=1)
    y = y * _rms_scale(y) * wcn_ref[...]
    y = y * _sigmoid(y)
    branch_a = _dot(y.astype(_BF16), wco_ref[...])
    merged = (g_ref[:, 0:D_MODEL].astype(_F32) * branch_a
              + g_ref[:, D_MODEL:2 * D_MODEL].astype(_F32) * branch_b)
    x1_ref[cur] = x_ref[...] + _dot(merged.astype(_BF16), wo_ref[...])


def _const_spec(shape):
    return pl.BlockSpec(shape, lambda *_: (0,) * len(shape))


def _weight_spec(shape):
    return pl.BlockSpec(shape, lambda *_: (0,) * len(shape), pipeline_mode=pl.Buffered(1))


def _rope_constants(tm):
    half = ROPE_DIM // 2
    inv_freq = ROPE_THETA ** (-jnp.arange(0, ROPE_DIM, 2, dtype=_F32) / ROPE_DIM)
    freq = jnp.broadcast_to(inv_freq[:, None], (half, tm))
    d = jnp.arange(LANES) % HEAD_DIM
    rotary = d < ROPE_DIM
    hit = (rotary[None, :] & ((d % half)[None, :] == jnp.arange(half)[:, None])).astype(_F32)
    sign = jnp.where(d < half, -1.0, 1.0)[None, :]
    zero = jnp.zeros_like(hit)
    sel = jnp.concatenate([jnp.concatenate([hit, zero], axis=1),
                           jnp.concatenate([zero, hit * sign], axis=1)], axis=0).astype(_BF16)
    aux = jnp.stack([1.0 - rotary.astype(_F32), (d < half).astype(_F32)]
                    + [jnp.zeros((LANES,), _F32)] * 6)
    return freq, sel, aux


def _params(n_axes):
    return pltpu.CompilerParams(dimension_semantics=("arbitrary",) * n_axes, vmem_limit_bytes=VMEM_LIMIT)


def kernel(x, positions, w_norm_mix, w_in, b_gate, w_dw, b_dw, w_conv_norm, w_conv_out, q_norm_w, k_norm_w,
           w_attn_out, w_o, w_norm_mlp, w_mlp_in, w_mlp_out):
    batch, seq, d_model = x.shape
    depth = w_in.shape[0]
    assert d_model == D_MODEL and seq % ATTN_TILE == 0 and ATTN_TILE % TOKEN_TILE == 0
    n_tok = batch * seq
    tm = TOKEN_TILE
    n_tiles = n_tok // tm
    in_width = w_in.shape[-1]

    tm_in = INPROJ_TILE
    xf = x.reshape(n_tok, d_model)
    pos = positions.astype(_F32).reshape(n_tok // tm_in, 1, tm_in)
    freq, sel, aux = _rope_constants(tm_in)
    head = jnp.arange(GROUP_WIDTH) // HEAD_DIM
    seg = (head[:, None] == head[None, :]).astype(_BF16)

    def row_spec(width):
        return pl.BlockSpec((tm_in, width), lambda i: (i, 0))

    for layer in range(depth):
        qk_gain = jnp.stack([jnp.tile(q_norm_w[layer], GROUP_WIDTH // HEAD_DIM),
                             jnp.tile(k_norm_w[layer], GROUP_WIDTH // HEAD_DIM)]).astype(_F32)

        def grp_spec(dil):
            return pl.BlockSpec((tm_in // dil, dil * GROUP_WIDTH), lambda i: (i, 0))

        grp_specs = [grp_spec(dil) for dil in DILATIONS] * 3
        grp_shapes = [jax.ShapeDtypeStruct((n_tok // dil, dil * GROUP_WIDTH), _BF16) for dil in DILATIONS] * 3
        outs = pl.pallas_call(
            _inproj_kernel,
            grid=(n_tok // tm_in,),
            in_specs=[row_spec(d_model), pl.BlockSpec((1, 1, tm_in), lambda i: (i, 0, 0)),
                      _const_spec((1, d_model)),
                      _const_spec((d_model, in_width)), _const_spec((1, 2 * d_model)),
                      _const_spec((2, GROUP_WIDTH)), _const_spec((ROPE_DIM // 2, tm_in)),
                      _const_spec((ROPE_DIM, 2 * LANES)), _const_spec((8, LANES)),
                      _const_spec((GROUP_WIDTH, GROUP_WIDTH))],
            out_specs=[row_spec(CONV_WIDTH)] + grp_specs + [row_spec(2 * d_model)],
            out_shape=[jax.ShapeDtypeStruct((n_tok, CONV_WIDTH), _BF16)] + grp_shapes
                      + [jax.ShapeDtypeStruct((n_tok, 2 * d_model), _BF16)],
            scratch_shapes=[pltpu.VMEM((2, GROUP_WIDTH // LANES, tm_in, LANES), _F32)],
            compiler_params=_params(1),
            name="inproj",
        )(xf, pos, w_norm_mix[layer].reshape(1, d_model), w_in[layer].astype(_BF16),
          b_gate[layer].reshape(1, 2 * d_model), qk_gain, freq, sel, aux, seg)
        u, qkv, gates = outs[0], outs[1:10], outs[10]

        n_attn_tiles = seq // ATTN_TILE
        in_specs, operands = [], []
        for g, dil in enumerate(DILATIONS):
            rows, width = ATTN_TILE // dil, dil * GROUP_WIDTH
            spans_per_tile, spans_per_seq = rows // SPAN, seq // dil // SPAN
            cur = pl.BlockSpec((rows, width), lambda b, i: (b * n_attn_tiles + i, 0))
            prev = pl.BlockSpec(
                (SPAN, width),
                lambda b, i, n=spans_per_tile, s=spans_per_seq: (b * s + jnp.maximum(i * n - 1, 0), 0))
            in_specs += [cur, cur, prev, cur, prev]
            operands += [qkv[g], qkv[3 + g], qkv[3 + g], qkv[6 + g], qkv[6 + g]]
        scratch = [pltpu.VMEM((ATTN_WIDTH // LANES, ATTN_TILE, LANES), _F32)] * 3
        ow = pl.pallas_call(
            _attn_kernel,
            grid=(batch, n_attn_tiles),
            in_specs=in_specs,
            out_specs=pl.BlockSpec((ATTN_TILE, ATTN_WIDTH), lambda b, i: (b * n_attn_tiles + i, 0)),
            out_shape=jax.ShapeDtypeStruct((n_tok, ATTN_WIDTH), _BF16),
            scratch_shapes=scratch,
            compiler_params=_params(2),
            name="attention",
        )(*operands)

        n_lane_tiles, n_chunks = CONV_WIDTH // LANES, D_FF // FF_CHUNK

        def tile_spec(width):
            return pl.BlockSpec((tm, width), lambda i: (jnp.minimum(i, n_tiles - 1), 0))

        halo_spec = pl.BlockSpec(
            (CONV_HALO, CONV_WIDTH),
            lambda i: (jnp.maximum(jnp.minimum(i, n_tiles - 1) * (tm // CONV_HALO) - 1, 0), 0))
        xf = pl.pallas_call(
            functools.partial(_mix_mlp_kernel, tiles_per_seq=seq // tm, n_tiles=n_tiles),
            grid=(n_tiles + 1,),
            in_specs=[tile_spec(d_model), tile_spec(CONV_WIDTH), halo_spec, tile_spec(ATTN_WIDTH),
                      tile_spec(2 * d_model),
                      _const_spec((n_lane_tiles, CONV_KERNEL, LANES)), _const_spec((n_lane_tiles, 1, LANES)),
                      _const_spec((1, CONV_WIDTH)), _weight_spec((CONV_WIDTH, d_model)),
                      _weight_spec((ATTN_WIDTH, d_model)), _weight_spec((d_model, d_model)),
                      _const_spec((1, d_model)), _weight_spec((n_chunks, d_model, FF_CHUNK)),
                      _weight_spec((D_FF, d_model))],
            out_specs=pl.BlockSpec((tm, d_model), lambda i: (jnp.maximum(i - 1, 0), 0)),
            out_shape=jax.ShapeDtypeStruct((n_tok, d_model), _F32),
            scratch_shapes=[pltpu.VMEM((CONV_WIDTH // LANES, CONV_HALO + tm, LANES), _F32),
                            pltpu.VMEM((CONV_WIDTH // LANES, tm, LANES), _F32),
                            pltpu.VMEM((2, tm, d_model), _F32),
                            pltpu.VMEM((tm, d_model), _BF16)],
            compiler_params=_params(1),
            name="mixer_mlp",
        )(xf, u, u, ow, gates,
          w_dw[layer].reshape(CONV_KERNEL, n_lane_tiles, LANES).transpose(1, 0, 2),
          b_dw[layer].reshape(n_lane_tiles, 1, LANES),
          w_conv_norm[layer].reshape(1, CONV_WIDTH), w_conv_out[layer].astype(_BF16),
          w_attn_out[layer].astype(_BF16), w_o[layer].astype(_BF16),
          w_norm_mlp[layer].reshape(1, d_model),
          jnp.stack([w_mlp_in[layer][:, c * FF_CHUNK:(c + 1) * FF_CHUNK].astype(_BF16) for c in range(n_chunks)]),
          w_mlp_out[layer].astype(_BF16))

    return xf.reshape(batch, seq, d_model)
```

```python
import functools

import jax
import jax.numpy as jnp
from jax import lax
from jax.experimental import pallas as pl
from jax.experimental.pallas import tpu as pltpu

D_MODEL = 1024
HEAD_DIM = 64
HEADS_PER_GROUP = 4
GROUP_WIDTH = HEADS_PER_GROUP * HEAD_DIM
DILATIONS = (1, 4, 16)
SPAN = 128
N_GROUPS = len(DILATIONS)
ATTN_WIDTH = N_GROUPS * GROUP_WIDTH
ROPE_THETA = 500000.0
ROPE_DIM = HEAD_DIM // 4
CONV_WIDTH = D_MODEL
CONV_KERNEL = 31
D_FF = 4 * D_MODEL
NORM_EPS = 1e-6
MASK_VALUE = -1e30
LOG2_E = 1.4426950408889634

LANES = 128
CONV_HALO = 32
ATTN_TILE = SPAN * DILATIONS[-1]
TOKEN_TILE = 512
INPROJ_TILE = 1024
CONV_ROWS = 32
CONV_STRIDE = 4
FF_CHUNK = 2048
COMBINE_ROWS = 256
VMEM_LIMIT = 56 * 1024 * 1024

_F32 = jnp.float32
_BF16 = jnp.bfloat16


def _dot(a, b):
    return jnp.dot(a, b, preferred_element_type=_F32)


def _sigmoid(z):
    return 0.5 * jnp.tanh(0.5 * z) + 0.5


def _rms_scale(xf):
    return lax.rsqrt(jnp.mean(xf * xf, axis=-1, keepdims=True) + NORM_EPS)


def _inproj_kernel(x_ref, pos_ref, wn_ref, w_ref, bg_ref, qkw_ref, freq_ref, sel_ref, aux_ref, seg_ref,
                   u_ref, q1_ref, q2_ref, q3_ref, k1_ref, k2_ref, k3_ref, v1_ref, v2_ref, v3_ref, g_ref,
                   z_ref):
    tm = x_ref.shape[0]
    x = x_ref[...]
    n = (x * wn_ref[...]).astype(_BF16)
    r = _rms_scale(x)

    ang = freq_ref[...] * pos_ref[0]
    cs = jnp.concatenate([jnp.cos(ang), jnp.sin(ang)], axis=0).T
    cs_hi = cs.astype(_BF16)
    cs_lo = (cs - cs_hi.astype(_F32)).astype(_BF16)
    tables = _dot(cs_hi, sel_ref[...]) + _dot(cs_lo, sel_ref[...])
    cos = tables[:, 0:LANES] + aux_ref[0:1, :]
    sin = tables[:, LANES:2 * LANES]
    first_half = aux_ref[1:2, :] > 0.5
    seg = seg_ref[...]

    def store_by_residue(slot, out_ref, dil):
        for t in range(GROUP_WIDTH // LANES):
            for r in range(dil):
                rows = z_ref[slot, t] if dil == 1 else z_ref[slot, t, pl.ds(r, tm // dil, stride=dil), :]
                c = r * GROUP_WIDTH + t * LANES
                out_ref[:, c:c + LANES] = rows.astype(_BF16)

    def finish_qk(z, slot, gain_row, scale, out_ref, dil):
        z = z * r
        ss = _dot((z * z).astype(_BF16), seg)
        z = z * lax.rsqrt(ss * (1.0 / HEAD_DIM) + NORM_EPS) * qkw_ref[gain_row:gain_row + 1, :]
        for t in range(GROUP_WIDTH // LANES):
            zt = z[:, t * LANES:(t + 1) * LANES]
            partner = jnp.where(first_half, pltpu.roll(zt, LANES - ROPE_DIM // 2, 1),
                                pltpu.roll(zt, ROPE_DIM // 2, 1))
            z_ref[slot, t] = (zt * cos + partner * sin) * scale
        store_by_residue(slot, out_ref, dil)

    def finish_v(z, slot, out_ref, dil):
        for t in range(GROUP_WIDTH // LANES):
            z_ref[slot, t] = z[:, t * LANES:(t + 1) * LANES] * r
        store_by_residue(slot, out_ref, dil)

    def finish_gate(z, t):
        z = z * r + bg_ref[:, t * D_MODEL:(t + 1) * D_MODEL]
        g_ref[:, t * D_MODEL:(t + 1) * D_MODEL] = _sigmoid(z).astype(_BF16)

    def finish_glu(z):
        z = z * r
        u_ref[...] = (z[:, 0:CONV_WIDTH] * _sigmoid(z[:, CONV_WIDTH:2 * CONV_WIDTH])).astype(_BF16)

    c0 = 2 * CONV_WIDTH
    c3 = c0 + 3 * ATTN_WIDTH
    q_refs = (q1_ref, q2_ref, q3_ref)
    k_refs = (k1_ref, k2_ref, k3_ref)
    v_refs = (v1_ref, v2_ref, v3_ref)

    segments = [(c3, D_MODEL, functools.partial(finish_gate, t=0)),
                (c3 + D_MODEL, D_MODEL, functools.partial(finish_gate, t=1))]
    for g, dil in enumerate(DILATIONS):
        segments.append((c0 + g * GROUP_WIDTH, GROUP_WIDTH,
                         functools.partial(finish_qk, slot=0, gain_row=0, scale=HEAD_DIM ** -0.5 * LOG2_E,
                                           out_ref=q_refs[g], dil=dil)))
        segments.append((c0 + ATTN_WIDTH + g * GROUP_WIDTH, GROUP_WIDTH,
                         functools.partial(finish_qk, slot=1, gain_row=1, scale=1.0,
                                           out_ref=k_refs[g], dil=dil)))
    segments.append((0, 2 * CONV_WIDTH, finish_glu))
    for g, dil in enumerate(DILATIONS):
        segments.append((c0 + 2 * ATTN_WIDTH + g * GROUP_WIDTH, GROUP_WIDTH,
                         functools.partial(finish_v, slot=g % 2, out_ref=v_refs[g], dil=dil)))

    pending = None
    for col, width, finish in segments:
        z = _dot(n, w_ref[:, col:col + width])
        if pending is not None:
            pending()
        pending = functools.partial(finish, z)
    pending()


def _attn_kernel(q1_ref, k1_ref, k1p_ref, v1_ref, v1p_ref,
                 q2_ref, k2_ref, k2p_ref, v2_ref, v2p_ref,
                 q3_ref, k3_ref, k3p_ref, v3_ref, v3p_ref,
                 ow_ref,
                 o_scr, l_scr, m_scr):
    tile = pl.program_id(1)

    qi = lax.broadcasted_iota(jnp.int32, (SPAN, 2 * SPAN), 0)
    ki = lax.broadcasted_iota(jnp.int32, (SPAN, 2 * SPAN), 1)
    dist = qi + SPAN - ki
    band = jnp.where((dist >= 0) & (dist <= SPAN), 0.0, MASK_VALUE).astype(_F32)
    band_first = jnp.where((ki >= SPAN) | (tile > 0), band, MASK_VALUE)
    band2 = jnp.concatenate([band, band], axis=0)
    band_first2 = jnp.concatenate([band_first, band_first], axis=0)

    lane = lax.broadcasted_iota(jnp.int32, (SPAN, LANES), 1)
    low_head = lane < HEAD_DIM
    ones = jnp.ones((2 * SPAN, LANES), _BF16)

    def with_prev(cur_ref, prev_ref, j, cols):
        if j == 0:
            return jnp.concatenate([prev_ref[:, cols], cur_ref[0:SPAN, cols]], axis=0)
        return cur_ref[(j - 1) * SPAN:(j + 1) * SPAN, cols]

    def unit(refs, j, col0, tok_rows, out_tile0):
        q_ref, k_ref, kp_ref, v_ref, vp_ref = refs
        bias2 = band_first2 if j == 0 else band2
        for pair in range(GROUP_WIDTH // LANES):
            cols = slice(col0 + pair * LANES, col0 + (pair + 1) * LANES)
            q = q_ref[j * SPAN:(j + 1) * SPAN, cols]
            kk = with_prev(k_ref, kp_ref, j, cols)
            vv = with_prev(v_ref, vp_ref, j, cols)
            zero = jnp.zeros_like(q)
            q2 = jnp.concatenate([jnp.where(low_head, q, zero), jnp.where(low_head, zero, q)], axis=0)
            s = lax.dot_general(q2, kk, (((1,), (1,)), ((), ())), preferred_element_type=_F32) + bias2
            m = jnp.max(s, axis=-1, keepdims=True)
            p = jnp.exp2(s - m).astype(_BF16)
            o2 = _dot(p, jnp.concatenate([vv, ones], axis=1))
            m2 = jnp.broadcast_to(m, (2 * SPAN, LANES))
            ot = out_tile0 + pair
            o_scr[ot, tok_rows, :] = jnp.where(low_head, o2[0:SPAN, 0:LANES], o2[SPAN:2 * SPAN, 0:LANES])
            l_scr[ot, tok_rows, :] = jnp.where(low_head, o2[0:SPAN, LANES:2 * LANES],
                                               o2[SPAN:2 * SPAN, LANES:2 * LANES])
            m_scr[ot, tok_rows, :] = jnp.where(low_head, m2[0:SPAN], m2[SPAN:2 * SPAN])

    group_refs = ((q1_ref, k1_ref, k1p_ref, v1_ref, v1p_ref),
                  (q2_ref, k2_ref, k2p_ref, v2_ref, v2p_ref),
                  (q3_ref, k3_ref, k3p_ref, v3_ref, v3p_ref))
    for g, dil in reversed(list(enumerate(DILATIONS))):
        for r in range(dil):
            for j in range(ATTN_TILE // dil // SPAN):
                tok_rows = pl.ds(dil * SPAN * j + r, SPAN) if dil == 1 else pl.ds(dil * SPAN * j + r, SPAN, stride=dil)
                unit(group_refs[g], j, r * GROUP_WIDTH, tok_rows, g * (GROUP_WIDTH // LANES))

    def combine(i, carry):
        rows = pl.ds(pl.multiple_of(i * COMBINE_ROWS, COMBINE_ROWS), COMBINE_ROWS)
        tiles_per_group = GROUP_WIDTH // LANES
        for t in range(tiles_per_group):
            tiles = [g * tiles_per_group + t for g in range(N_GROUPS)]
            ms = [m_scr[ot, rows, :] for ot in tiles]
            mx = jnp.maximum(jnp.maximum(ms[0], ms[1]), ms[2])
            ws = [jnp.exp2(m - mx) for m in ms]
            den = sum(l_scr[ot, rows, :] * w for ot, w in zip(tiles, ws))
            inv = 1.0 / den
            for ot, w in zip(tiles, ws):
                ow_ref[rows, ot * LANES:(ot + 1) * LANES] = (o_scr[ot, rows, :] * (w * inv)).astype(_BF16)
        return carry

    lax.fori_loop(0, ATTN_TILE // COMBINE_ROWS, combine, 0)


def _mix_mlp_kernel(x_ref, u_ref, uh_ref, ow_ref, g_ref,
                    wdw_ref, bdw_ref, wcn_ref, wco_ref, wao_ref, wo_ref, wnm_ref, w1_ref, w2_ref,
                    out_ref, ext_ref, y_ref, x1_ref, hn_ref, *, tiles_per_seq, n_tiles):
    step = pl.program_id(0)
    tile = jnp.minimum(step, n_tiles - 1)
    first = (tile % tiles_per_seq) == 0
    tm = x_ref.shape[0]
    cur = step % 2
    prev = 1 - cur

    @pl.when(step == 0)
    def _():
        x1_ref[1] = jnp.zeros((tm, D_MODEL), _F32)

    shift = CONV_HALO - (CONV_KERNEL - 1)
    block = CONV_ROWS * CONV_STRIDE
    n_lane_tiles = CONV_WIDTH // LANES
    n_chunks = D_FF // FF_CHUNK
    lane_tiles_per_chunk = n_lane_tiles // n_chunks
    assert lane_tiles_per_chunk * n_chunks == n_lane_tiles

    for c in range(n_lane_tiles):
        cols = slice(c * LANES, (c + 1) * LANES)
        halo = uh_ref[:, cols].astype(_F32)
        ext_ref[c, 0:CONV_HALO, :] = jnp.where(first, jnp.zeros_like(halo), halo)
        ext_ref[c, CONV_HALO:, :] = u_ref[:, cols].astype(_F32)

    def conv_lane_tile(c):
        taps = [jnp.broadcast_to(wdw_ref[c, k:k + 1, :], (CONV_ROWS, LANES)) for k in range(CONV_KERNEL)]
        bias = jnp.broadcast_to(bdw_ref[c], (CONV_ROWS, LANES))

        for base in range(0, tm, block):
            accs = [bias] * CONV_STRIDE
            for s in range(CONV_KERNEL + CONV_STRIDE - 1):
                rows = ext_ref[c, pl.ds(base + shift + s, CONV_ROWS, stride=CONV_STRIDE), :]
                for phase in range(CONV_STRIDE):
                    if 0 <= s - phase < CONV_KERNEL:
                        accs[phase] = accs[phase] + rows * taps[s - phase]
            for phase in range(CONV_STRIDE):
                y_ref[c, pl.ds(base + phase, CONV_ROWS, stride=CONV_STRIDE), :] = accs[phase]

    xp = x1_ref[prev]
    hn_ref[...] = (xp * _rms_scale(xp) * wnm_ref[...]).astype(_BF16)
    out_ref[...] = xp

    def chunk_body(c, carry):
        h = jnp.maximum(_dot(hn_ref[...], w1_ref[c]), 0.0)
        rows = pl.ds(pl.multiple_of(c * FF_CHUNK, FF_CHUNK), FF_CHUNK)
        out_ref[...] += _dot((h * h).astype(_BF16), w2_ref[rows, :])
        for t in range(lane_tiles_per_chunk):
            conv_lane_tile(c * lane_tiles_per_chunk + t)
        return carry

    lax.fori_loop(0, n_chunks, chunk_body, 0)

    branch_b = _dot(ow_ref[...], wao_ref[...])
    y = jnp.concatenate([y_ref[c] for c in range(n_lane_tiles)], axis=1)
    y = y * _rms_scale(y) * wcn_ref[...]
    y = y * _sigmoid(y)
    branch_a = _dot(y.astype(_BF16), wco_ref[...])
    merged = (g_ref[:, 0:D_MODEL].astype(_F32) * branch_a
              + g_ref[:, D_MODEL:2 * D_MODEL].astype(_F32) * branch_b)
    x1_ref[cur] = x_ref[...] + _dot(merged.astype(_BF16), wo_ref[...])


def _const_spec(shape):
    return pl.BlockSpec(shape, lambda *_: (0,) * len(shape))


def _weight_spec(shape):
    return pl.BlockSpec(shape, lambda *_: (0,) * len(shape), pipeline_mode=pl.Buffered(1))


def _rope_constants(tm):
    half = ROPE_DIM // 2
    inv_freq = ROPE_THETA ** (-jnp.arange(0, ROPE_DIM, 2, dtype=_F32) / ROPE_DIM)
    freq = jnp.broadcast_to(inv_freq[:, None], (half, tm))
    d = jnp.arange(LANES) % HEAD_DIM
    rotary = d < ROPE_DIM
    hit = (rotary[None, :] & ((d % half)[None, :] == jnp.arange(half)[:, None])).astype(_F32)
    sign = jnp.where(d < half, -1.0, 1.0)[None, :]
    zero = jnp.zeros_like(hit)
    sel = jnp.concatenate([jnp.concatenate([hit, zero], axis=1),
                           jnp.concatenate([zero, hit * sign], axis=1)], axis=0).astype(_BF16)
    aux = jnp.stack([1.0 - rotary.astype(_F32), (d < half).astype(_F32)]
                    + [jnp.zeros((LANES,), _F32)] * 6)
    return freq, sel, aux


def _params(n_axes):
    return pltpu.CompilerParams(dimension_semantics=("arbitrary",) * n_axes, vmem_limit_bytes=VMEM_LIMIT)


def kernel(x, positions, w_norm_mix, w_in, b_gate, w_dw, b_dw, w_conv_norm, w_conv_out, q_norm_w, k_norm_w,
           w_attn_out, w_o, w_norm_mlp, w_mlp_in, w_mlp_out):
    batch, seq, d_model = x.shape
    depth = w_in.shape[0]
    assert d_model == D_MODEL and seq % ATTN_TILE == 0 and ATTN_TILE % TOKEN_TILE == 0
    n_tok = batch * seq
    tm = TOKEN_TILE
    n_tiles = n_tok // tm
    in_width = w_in.shape[-1]

    tm_in = INPROJ_TILE
    xf = x.reshape(n_tok, d_model)
    pos = positions.astype(_F32).reshape(n_tok // tm_in, 1, tm_in)
    freq, sel, aux = _rope_constants(tm_in)
    head = jnp.arange(GROUP_WIDTH) // HEAD_DIM
    seg = (head[:, None] == head[None, :]).astype(_BF16)

    def row_spec(width):
        return pl.BlockSpec((tm_in, width), lambda i: (i, 0))

    for layer in range(depth):
        qk_gain = jnp.stack([jnp.tile(q_norm_w[layer], GROUP_WIDTH // HEAD_DIM),
                             jnp.tile(k_norm_w[layer], GROUP_WIDTH // HEAD_DIM)]).astype(_F32)

        def grp_spec(dil):
            return pl.BlockSpec((tm_in // dil, dil * GROUP_WIDTH), lambda i: (i, 0))

        grp_specs = [grp_spec(dil) for dil in DILATIONS] * 3
        grp_shapes = [jax.ShapeDtypeStruct((n_tok // dil, dil * GROUP_WIDTH), _BF16) for dil in DILATIONS] * 3
        outs = pl.pallas_call(
            _inproj_kernel,
            grid=(n_tok // tm_in,),
            in_specs=[row_spec(d_model), pl.BlockSpec((1, 1, tm_in), lambda i: (i, 0, 0)),
                      _const_spec((1, d_model)),
                      _const_spec((d_model, in_width)), _const_spec((1, 2 * d_model)),
                      _const_spec((2, GROUP_WIDTH)), _const_spec((ROPE_DIM // 2, tm_in)),
                      _const_spec((ROPE_DIM, 2 * LANES)), _const_spec((8, LANES)),
                      _const_spec((GROUP_WIDTH, GROUP_WIDTH))],
            out_specs=[row_spec(CONV_WIDTH)] + grp_specs + [row_spec(2 * d_model)],
            out_shape=[jax.ShapeDtypeStruct((n_tok, CONV_WIDTH), _BF16)] + grp_shapes
                      + [jax.ShapeDtypeStruct((n_tok, 2 * d_model), _BF16)],
            scratch_shapes=[pltpu.VMEM((2, GROUP_WIDTH // LANES, tm_in, LANES), _F32)],
            compiler_params=_params(1),
            name="inproj",
        )(xf, pos, w_norm_mix[layer].reshape(1, d_model), w_in[layer].astype(_BF16),
          b_gate[layer].reshape(1, 2 * d_model), qk_gain, freq, sel, aux, seg)
        u, qkv, gates = outs[0], outs[1:10], outs[10]

        n_attn_tiles = seq // ATTN_TILE
        in_specs, operands = [], []
        for g, dil in enumerate(DILATIONS):
            rows, width = ATTN_TILE // dil, dil * GROUP_WIDTH
            spans_per_tile, spans_per_seq = rows // SPAN, seq // dil // SPAN
            cur = pl.BlockSpec((rows, width), lambda b, i: (b * n_attn_tiles + i, 0))
            prev = pl.BlockSpec(
                (SPAN, width),
                lambda b, i, n=spans_per_tile, s=spans_per_seq: (b * s + jnp.maximum(i * n - 1, 0), 0))
            in_specs += [cur, cur, prev, cur, prev]
            operands += [qkv[g], qkv[3 + g], qkv[3 + g], qkv[6 + g], qkv[6 + g]]
        scratch = [pltpu.VMEM((ATTN_WIDTH // LANES, ATTN_TILE, LANES), _F32)] * 3
        ow = pl.pallas_call(
            _attn_kernel,
            grid=(batch, n_attn_tiles),
            in_specs=in_specs,
            out_specs=pl.BlockSpec((ATTN_TILE, ATTN_WIDTH), lambda b, i: (b * n_attn_tiles + i, 0)),
            out_shape=jax.ShapeDtypeStruct((n_tok, ATTN_WIDTH), _BF16),
            scratch_shapes=scratch,
            compiler_params=_params(2),
            name="attention",
        )(*operands)

        n_lane_tiles, n_chunks = CONV_WIDTH // LANES, D_FF // FF_CHUNK

        def tile_spec(width):
            return pl.BlockSpec((tm, width), lambda i: (jnp.minimum(i, n_tiles - 1), 0))

        halo_spec = pl.BlockSpec(
            (CONV_HALO, CONV_WIDTH),
            lambda i: (jnp.maximum(jnp.minimum(i, n_tiles - 1) * (tm // CONV_HALO) - 1, 0), 0))
        xf = pl.pallas_call(
            functools.partial(_mix_mlp_kernel, tiles_per_seq=seq // tm, n_tiles=n_tiles),
            grid=(n_tiles + 1,),
            in_specs=[tile_spec(d_model), tile_spec(CONV_WIDTH), halo_spec, tile_spec(ATTN_WIDTH),
                      tile_spec(2 * d_model),
                      _const_spec((n_lane_tiles, CONV_KERNEL, LANES)), _const_spec((n_lane_tiles, 1, LANES)),
                      _const_spec((1, CONV_WIDTH)), _weight_spec((CONV_WIDTH, d_model)),
                      _weight_spec((ATTN_WIDTH, d_model)), _weight_spec((d_model, d_model)),
                      _const_spec((1, d_model)), _weight_spec((n_chunks, d_model, FF_CHUNK)),
                      _weight_spec((D_FF, d_model))],
            out_specs=pl.BlockSpec((tm, d_model), lambda i: (jnp.maximum(i - 1, 0), 0)),
            out_shape=jax.ShapeDtypeStruct((n_tok, d_model), _F32),
            scratch_shapes=[pltpu.VMEM((CONV_WIDTH // LANES, CONV_HALO + tm, LANES), _F32),
                            pltpu.VMEM((CONV_WIDTH // LANES, tm, LANES), _F32),
                            pltpu.VMEM((2, tm, d_model), _F32),
                            pltpu.VMEM((tm, d_model), _BF16)],
            compiler_params=_params(1),
            name="mixer_mlp",
        )(xf, u, u, ow, gates,
          w_dw[layer].reshape(CONV_KERNEL, n_lane_tiles, LANES).transpose(1, 0, 2),
          b_dw[layer].reshape(n_lane_tiles, 1, LANES),
          w_conv_norm[layer].reshape(1, CONV_WIDTH), w_conv_out[layer].astype(_BF16),
          w_attn_out[layer].astype(_BF16), w_o[layer].astype(_BF16),
          w_norm_mlp[layer].reshape(1, d_model),
          jnp.stack([w_mlp_in[layer][:, c * FF_CHUNK:(c + 1) * FF_CHUNK].astype(_BF16) for c in range(n_chunks)]),
          w_mlp_out[layer].astype(_BF16))

    return xf.reshape(batch, seq, d_model)
```

```python
import functools

import jax
import jax.numpy as jnp
from jax import lax
from jax.experimental import pallas as pl
from jax.experimental.pallas import tpu as pltpu

D_MODEL = 1024
HEAD_DIM = 64
HEADS_PER_GROUP = 4
GROUP_WIDTH = HEADS_PER_GROUP * HEAD_DIM
DILATIONS = (1, 4, 16)
SPAN = 128
N_GROUPS = len(DILATIONS)
ATTN_WIDTH = N_GROUPS * GROUP_WIDTH
ROPE_THETA = 500000.0
ROPE_DIM = HEAD_DIM // 4
CONV_WIDTH = D_MODEL
CONV_KERNEL = 31
D_FF = 4 * D_MODEL
NORM_EPS = 1e-6
MASK_VALUE = -1e30
LOG2_E = 1.4426950408889634

LANES = 128
CONV_HALO = 32
ATTN_TILE = SPAN * DILATIONS[-1]
TOKEN_TILE = 512
INPROJ_TILE = 1024
CONV_ROWS = 32
CONV_STRIDE = 4
FF_CHUNK = 2048
COMBINE_ROWS = 256
VMEM_LIMIT = 56 * 1024 * 1024
ATTN_VMEM_LIMIT = 52 * 1024 * 1024

_F32 = jnp.float32
_BF16 = jnp.bfloat16


def _dot(a, b):
    return jnp.dot(a, b, preferred_element_type=_F32)


def _sigmoid(z):
    return 0.5 * jnp.tanh(0.5 * z) + 0.5


def _rms_scale(xf):
    return lax.rsqrt(jnp.mean(xf * xf, axis=-1, keepdims=True) + NORM_EPS)


def _inproj_kernel(x_ref, pos_ref, wn_ref, w_ref, bg_ref, qkw_ref, freq_ref, sel_ref, aux_ref, seg_ref,
                   u_ref, q1_ref, q2_ref, q3_ref, k1_ref, k2_ref, k3_ref, v1_ref, v2_ref, v3_ref, g_ref,
                   z_ref):
    tm = x_ref.shape[0]
    x = x_ref[...]
    n = (x * wn_ref[...]).astype(_BF16)
    r = _rms_scale(x)

    ang = freq_ref[...] * pos_ref[0]
    cs = jnp.concatenate([jnp.cos(ang), jnp.sin(ang)], axis=0).T
    cs_hi = cs.astype(_BF16)
    cs_lo = (cs - cs_hi.astype(_F32)).astype(_BF16)
    tables = _dot(cs_hi, sel_ref[...]) + _dot(cs_lo, sel_ref[...])
    cos = tables[:, 0:LANES] + aux_ref[0:1, :]
    sin = tables[:, LANES:2 * LANES]
    first_half = aux_ref[1:2, :] > 0.5
    seg = seg_ref[...]

    def store_by_residue(slot, out_ref, dil):
        for t in range(GROUP_WIDTH // LANES):
            for r in range(dil):
                rows = z_ref[slot, t] if dil == 1 else z_ref[slot, t, pl.ds(r, tm // dil, stride=dil), :]
                c = r * GROUP_WIDTH + t * LANES
                out_ref[:, c:c + LANES] = rows.astype(_BF16)

    def finish_qk(z, slot, gain_row, scale, out_ref, dil):
        z = z * r
        ss = _dot((z * z).astype(_BF16), seg)
        z = z * lax.rsqrt(ss * (1.0 / HEAD_DIM) + NORM_EPS) * qkw_ref[gain_row:gain_row + 1, :]
        for t in range(GROUP_WIDTH // LANES):
            zt = z[:, t * LANES:(t + 1) * LANES]
            partner = jnp.where(first_half, pltpu.roll(zt, LANES - ROPE_DIM // 2, 1),
                                pltpu.roll(zt, ROPE_DIM // 2, 1))
            z_ref[slot, t] = (zt * cos + partner * sin) * scale
        store_by_residue(slot, out_ref, dil)

    def finish_v(z, slot, out_ref, dil):
        for t in range(GROUP_WIDTH // LANES):
            z_ref[slot, t] = z[:, t * LANES:(t + 1) * LANES] * r
        store_by_residue(slot, out_ref, dil)

    def finish_gate(z, t):
        z = z * r + bg_ref[:, t * D_MODEL:(t + 1) * D_MODEL]
        g_ref[:, t * D_MODEL:(t + 1) * D_MODEL] = _sigmoid(z).astype(_BF16)

    def finish_glu(z):
        z = z * r
        u_ref[...] = (z[:, 0:CONV_WIDTH] * _sigmoid(z[:, CONV_WIDTH:2 * CONV_WIDTH])).astype(_BF16)

    c0 = 2 * CONV_WIDTH
    c3 = c0 + 3 * ATTN_WIDTH
    q_refs = (q1_ref, q2_ref, q3_ref)
    k_refs = (k1_ref, k2_ref, k3_ref)
    v_refs = (v1_ref, v2_ref, v3_ref)

    segments = [(c3, D_MODEL, functools.partial(finish_gate, t=0)),
                (c3 + D_MODEL, D_MODEL, functools.partial(finish_gate, t=1))]
    for g, dil in enumerate(DILATIONS):
        segments.append((c0 + g * GROUP_WIDTH, GROUP_WIDTH,
                         functools.partial(finish_qk, slot=0, gain_row=0, scale=HEAD_DIM ** -0.5 * LOG2_E,
                                           out_ref=q_refs[g], dil=dil)))
        segments.append((c0 + ATTN_WIDTH + g * GROUP_WIDTH, GROUP_WIDTH,
                         functools.partial(finish_qk, slot=1, gain_row=1, scale=1.0,
                                           out_ref=k_refs[g], dil=dil)))
    segments.append((0, 2 * CONV_WIDTH, finish_glu))
    for g, dil in enumerate(DILATIONS):
        segments.append((c0 + 2 * ATTN_WIDTH + g * GROUP_WIDTH, GROUP_WIDTH,
                         functools.partial(finish_v, slot=g % 2, out_ref=v_refs[g], dil=dil)))

    pending = None
    for col, width, finish in segments:
        z = _dot(n, w_ref[:, col:col + width])
        if pending is not None:
            pending()
        pending = functools.partial(finish, z)
    pending()


def _attn_kernel(q1_ref, k1_ref, k1p_ref, v1_ref, v1p_ref,
                 q2_ref, k2_ref, k2p_ref, v2_ref, v2p_ref,
                 q3_ref, k3_ref, k3p_ref, v3_ref, v3p_ref,
                 ow_ref,
                 o_scr, l_scr, m_scr):
    tile = pl.program_id(1)

    qi = lax.broadcasted_iota(jnp.int32, (SPAN, 2 * SPAN), 0)
    ki = lax.broadcasted_iota(jnp.int32, (SPAN, 2 * SPAN), 1)
    dist = qi + SPAN - ki
    band = jnp.where((dist >= 0) & (dist <= SPAN), 0.0, MASK_VALUE).astype(_F32)
    band_first = jnp.where((ki >= SPAN) | (tile > 0), band, MASK_VALUE)
    band2 = jnp.concatenate([band, band], axis=0)
    band_first2 = jnp.concatenate([band_first, band_first], axis=0)

    lane = lax.broadcasted_iota(jnp.int32, (SPAN, LANES), 1)
    low_head = lane < HEAD_DIM
    ones = jnp.ones((2 * SPAN, LANES), _BF16)

    def with_prev(cur_ref, prev_ref, j, cols):
        if j == 0:
            return jnp.concatenate([prev_ref[:, cols], cur_ref[0:SPAN, cols]], axis=0)
        return cur_ref[(j - 1) * SPAN:(j + 1) * SPAN, cols]

    def unit(refs, j, col0, tok_rows, out_tile0):
        q_ref, k_ref, kp_ref, v_ref, vp_ref = refs
        bias2 = band_first2 if j == 0 else band2
        for pair in range(GROUP_WIDTH // LANES):
            cols = slice(col0 + pair * LANES, col0 + (pair + 1) * LANES)
            q = q_ref[j * SPAN:(j + 1) * SPAN, cols]
            kk = with_prev(k_ref, kp_ref, j, cols)
            vv = with_prev(v_ref, vp_ref, j, cols)
            zero = jnp.zeros_like(q)
            q2 = jnp.concatenate([jnp.where(low_head, q, zero), jnp.where(low_head, zero, q)], axis=0)
            s = lax.dot_general(q2, kk, (((1,), (1,)), ((), ())), preferred_element_type=_F32) + bias2
            m = jnp.max(s, axis=-1, keepdims=True)
            p = jnp.exp2(s - m).astype(_BF16)
            o2 = _dot(p, jnp.concatenate([vv, ones], axis=1))
            m2 = jnp.broadcast_to(m, (2 * SPAN, LANES))
            ot = out_tile0 + pair
            o_scr[ot, tok_rows, :] = jnp.where(low_head, o2[0:SPAN, 0:LANES], o2[SPAN:2 * SPAN, 0:LANES])
            l_scr[ot, tok_rows, :] = jnp.where(low_head, o2[0:SPAN, LANES:2 * LANES],
                                               o2[SPAN:2 * SPAN, LANES:2 * LANES])
            m_scr[ot, tok_rows, :] = jnp.where(low_head, m2[0:SPAN], m2[SPAN:2 * SPAN])

    group_refs = ((q1_ref, k1_ref, k1p_ref, v1_ref, v1p_ref),
                  (q2_ref, k2_ref, k2p_ref, v2_ref, v2p_ref),
                  (q3_ref, k3_ref, k3p_ref, v3_ref, v3p_ref))
    for g, dil in enumerate(DILATIONS):
        for r in range(dil):
            for j in range(ATTN_TILE // dil // SPAN):
                tok_rows = pl.ds(dil * SPAN * j + r, SPAN) if dil == 1 else pl.ds(dil * SPAN * j + r, SPAN, stride=dil)
                unit(group_refs[g], j, r * GROUP_WIDTH, tok_rows, g * (GROUP_WIDTH // LANES))

    def combine(i, carry):
        rows = pl.ds(pl.multiple_of(i * COMBINE_ROWS, COMBINE_ROWS), COMBINE_ROWS)
        tiles_per_group = GROUP_WIDTH // LANES
        for t in range(tiles_per_group):
            tiles = [g * tiles_per_group + t for g in range(N_GROUPS)]
            ms = [m_scr[ot, rows, :] for ot in tiles]
            mx = jnp.maximum(jnp.maximum(ms[0], ms[1]), ms[2])
            ws = [jnp.exp2(m - mx) for m in ms]
            den = sum(l_scr[ot, rows, :] * w for ot, w in zip(tiles, ws))
            inv = 1.0 / den
            for ot, w in zip(tiles, ws):
                ow_ref[rows, ot * LANES:(ot + 1) * LANES] = (o_scr[ot, rows, :] * (w * inv)).astype(_BF16)
        return carry

    lax.fori_loop(0, ATTN_TILE // COMBINE_ROWS, combine, 0)


def _mix_mlp_kernel(x_ref, u_ref, uh_ref, ow_ref, g_ref,
                    wdw_ref, bdw_ref, wcn_ref, wco_ref, wao_ref, wo_ref, wnm_ref, w1_ref, w2_ref,
                    out_ref, ext_ref, y_ref, x1_ref, hn_ref, *, tiles_per_seq, n_tiles):
    step = pl.program_id(0)
    tile = jnp.minimum(step, n_tiles - 1)
    first = (tile % tiles_per_seq) == 0
    tm = x_ref.shape[0]
    cur = step % 2
    prev = 1 - cur

    @pl.when(step == 0)
    def _():
        x1_ref[1] = jnp.zeros((tm, D_MODEL), _F32)

    shift = CONV_HALO - (CONV_KERNEL - 1)
    block = CONV_ROWS * CONV_STRIDE
    n_lane_tiles = CONV_WIDTH // LANES
    n_chunks = D_FF // FF_CHUNK
    lane_tiles_per_chunk = n_lane_tiles // n_chunks
    assert lane_tiles_per_chunk * n_chunks == n_lane_tiles

    for c in range(n_lane_tiles):
        cols = slice(c * LANES, (c + 1) * LANES)
        halo = uh_ref[:, cols].astype(_F32)
        ext_ref[c, 0:CONV_HALO, :] = jnp.where(first, jnp.zeros_like(halo), halo)
        ext_ref[c, CONV_HALO:, :] = u_ref[:, cols].astype(_F32)

    def conv_lane_tile(c):
        taps = [jnp.broadcast_to(wdw_ref[c, k:k + 1, :], (CONV_ROWS, LANES)) for k in range(CONV_KERNEL)]
        bias = jnp.broadcast_to(bdw_ref[c], (CONV_ROWS, LANES))

        for base in range(0, tm, block):
            accs = [bias] * CONV_STRIDE
            for s in range(CONV_KERNEL + CONV_STRIDE - 1):
                rows = ext_ref[c, pl.ds(base + shift + s, CONV_ROWS, stride=CONV_STRIDE), :]
                for phase in range(CONV_STRIDE):
                    if 0 <= s - phase < CONV_KERNEL:
                        accs[phase] = accs[phase] + rows * taps[s - phase]
            for phase in range(CONV_STRIDE):
                y_ref[c, pl.ds(base + phase, CONV_ROWS, stride=CONV_STRIDE), :] = accs[phase]

    xp = x1_ref[prev]
    hn_ref[...] = (xp * _rms_scale(xp) * wnm_ref[...]).astype(_BF16)
    out_ref[...] = xp

    def chunk_body(c, carry):
        h = jnp.maximum(_dot(hn_ref[...], w1_ref[c]), 0.0)
        rows = pl.ds(pl.multiple_of(c * FF_CHUNK, FF_CHUNK), FF_CHUNK)
        out_ref[...] += _dot((h * h).astype(_BF16), w2_ref[rows, :])
        for t in range(lane_tiles_per_chunk):
            conv_lane_tile(c * lane_tiles_per_chunk + t)
        return carry

    lax.fori_loop(0, n_chunks, chunk_body, 0)

    branch_b = _dot(ow_ref[...], wao_ref[...])
    y = jnp.concatenate([y_ref[c] for c in range(n_lane_tiles)], axis=1)
    y = y * _rms_scale(y) * wcn_ref[...]
    y = y * _sigmoid(y)
    branch_a = _dot(y.astype(_BF16), wco_ref[...])
    merged = (g_ref[:, 0:D_MODEL].astype(_F32) * branch_a
              + g_ref[:, D_MODEL:2 * D_MODEL].astype(_F32) * branch_b)
    x1_ref[cur] = x_ref[...] + _dot(merged.astype(_BF16), wo_ref[...])


def _const_spec(shape):
    return pl.BlockSpec(shape, lambda *_: (0,) * len(shape))


def _weight_spec(shape):
    return pl.BlockSpec(shape, lambda *_: (0,) * len(shape), pipeline_mode=pl.Buffered(1))


def _rope_constants(tm):
    half = ROPE_DIM // 2
    inv_freq = ROPE_THETA ** (-jnp.arange(0, ROPE_DIM, 2, dtype=_F32) / ROPE_DIM)
    freq = jnp.broadcast_to(inv_freq[:, None], (half, tm))
    d = jnp.arange(LANES) % HEAD_DIM
    rotary = d < ROPE_DIM
    hit = (rotary[None, :] & ((d % half)[None, :] == jnp.arange(half)[:, None])).astype(_F32)
    sign = jnp.where(d < half, -1.0, 1.0)[None, :]
    zero = jnp.zeros_like(hit)
    sel = jnp.concatenate([jnp.concatenate([hit, zero], axis=1),
                           jnp.concatenate([zero, hit * sign], axis=1)], axis=0).astype(_BF16)
    aux = jnp.stack([1.0 - rotary.astype(_F32), (d < half).astype(_F32)]
                    + [jnp.zeros((LANES,), _F32)] * 6)
    return freq, sel, aux


def _params(n_axes, vmem_limit=VMEM_LIMIT):
    return pltpu.CompilerParams(dimension_semantics=("arbitrary",) * n_axes, vmem_limit_bytes=vmem_limit)


def kernel(x, positions, w_norm_mix, w_in, b_gate, w_dw, b_dw, w_conv_norm, w_conv_out, q_norm_w, k_norm_w,
           w_attn_out, w_o, w_norm_mlp, w_mlp_in, w_mlp_out):
    batch, seq, d_model = x.shape
    depth = w_in.shape[0]
    assert d_model == D_MODEL and seq % ATTN_TILE == 0 and ATTN_TILE % TOKEN_TILE == 0
    n_tok = batch * seq
    tm = TOKEN_TILE
    n_tiles = n_tok // tm
    in_width = w_in.shape[-1]

    tm_in = INPROJ_TILE
    xf = x.reshape(n_tok, d_model)
    pos = positions.astype(_F32).reshape(n_tok // tm_in, 1, tm_in)
    freq, sel, aux = _rope_constants(tm_in)
    head = jnp.arange(GROUP_WIDTH) // HEAD_DIM
    seg = (head[:, None] == head[None, :]).astype(_BF16)

    def row_spec(width):
        return pl.BlockSpec((tm_in, width), lambda i: (i, 0))

    for layer in range(depth):
        qk_gain = jnp.stack([jnp.tile(q_norm_w[layer], GROUP_WIDTH // HEAD_DIM),
                             jnp.tile(k_norm_w[layer], GROUP_WIDTH // HEAD_DIM)]).astype(_F32)

        def grp_spec(dil):
            return pl.BlockSpec((tm_in // dil, dil * GROUP_WIDTH), lambda i: (i, 0))

        grp_specs = [grp_spec(dil) for dil in DILATIONS] * 3
        grp_shapes = [jax.ShapeDtypeStruct((n_tok // dil, dil * GROUP_WIDTH), _BF16) for dil in DILATIONS] * 3
        outs = pl.pallas_call(
            _inproj_kernel,
            grid=(n_tok // tm_in,),
            in_specs=[row_spec(d_model), pl.BlockSpec((1, 1, tm_in), lambda i: (i, 0, 0)),
                      _const_spec((1, d_model)),
                      _const_spec((d_model, in_width)), _const_spec((1, 2 * d_model)),
                      _const_spec((2, GROUP_WIDTH)), _const_spec((ROPE_DIM // 2, tm_in)),
                      _const_spec((ROPE_DIM, 2 * LANES)), _const_spec((8, LANES)),
                      _const_spec((GROUP_WIDTH, GROUP_WIDTH))],
            out_specs=[row_spec(CONV_WIDTH)] + grp_specs + [row_spec(2 * d_model)],
            out_shape=[jax.ShapeDtypeStruct((n_tok, CONV_WIDTH), _BF16)] + grp_shapes
                      + [jax.ShapeDtypeStruct((n_tok, 2 * d_model), _BF16)],
            scratch_shapes=[pltpu.VMEM((2, GROUP_WIDTH // LANES, tm_in, LANES), _F32)],
            compiler_params=_params(1),
            name="inproj",
        )(xf, pos, w_norm_mix[layer].reshape(1, d_model), w_in[layer].astype(_BF16),
          b_gate[layer].reshape(1, 2 * d_model), qk_gain, freq, sel, aux, seg)
        u, qkv, gates = outs[0], outs[1:10], outs[10]

        n_attn_tiles = seq // ATTN_TILE
        in_specs, operands = [], []
        for g, dil in enumerate(DILATIONS):
            rows, width = ATTN_TILE // dil, dil * GROUP_WIDTH
            spans_per_tile, spans_per_seq = rows // SPAN, seq // dil // SPAN
            cur = pl.BlockSpec((rows, width), lambda b, i: (b * n_attn_tiles + i, 0))
            prev = pl.BlockSpec(
                (SPAN, width),
                lambda b, i, n=spans_per_tile, s=spans_per_seq: (b * s + jnp.maximum(i * n - 1, 0), 0))
            in_specs += [cur, cur, prev, cur, prev]
            operands += [qkv[g], qkv[3 + g], qkv[3 + g], qkv[6 + g], qkv[6 + g]]
        scratch = [pltpu.VMEM((ATTN_WIDTH // LANES, ATTN_TILE, LANES), _F32)] * 3
        ow = pl.pallas_call(
            _attn_kernel,
            grid=(batch, n_attn_tiles),
            in_specs=in_specs,
            out_specs=pl.BlockSpec((ATTN_TILE, ATTN_WIDTH), lambda b, i: (b * n_attn_tiles + i, 0)),
            out_shape=jax.ShapeDtypeStruct((n_tok, ATTN_WIDTH), _BF16),
            scratch_shapes=scratch,
            compiler_params=_params(2, ATTN_VMEM_LIMIT),
            name="attention",
        )(*operands)

        n_lane_tiles, n_chunks = CONV_WIDTH // LANES, D_FF // FF_CHUNK

        def tile_spec(width):
            return pl.BlockSpec((tm, width), lambda i: (jnp.minimum(i, n_tiles - 1), 0))

        halo_spec = pl.BlockSpec(
            (CONV_HALO, CONV_WIDTH),
            lambda i: (jnp.maximum(jnp.minimum(i, n_tiles - 1) * (tm // CONV_HALO) - 1, 0), 0))
        xf = pl.pallas_call(
            functools.partial(_mix_mlp_kernel, tiles_per_seq=seq // tm, n_tiles=n_tiles),
            grid=(n_tiles + 1,),
            in_specs=[tile_spec(d_model), tile_spec(CONV_WIDTH), halo_spec, tile_spec(ATTN_WIDTH),
                      tile_spec(2 * d_model),
                      _const_spec((n_lane_tiles, CONV_KERNEL, LANES)), _const_spec((n_lane_tiles, 1, LANES)),
                      _const_spec((1, CONV_WIDTH)), _weight_spec((CONV_WIDTH, d_model)),
                      _weight_spec((ATTN_WIDTH, d_model)), _weight_spec((d_model, d_model)),
                      _const_spec((1, d_model)), _weight_spec((n_chunks, d_model, FF_CHUNK)),
                      _weight_spec((D_FF, d_model))],
            out_specs=pl.BlockSpec((tm, d_model), lambda i: (jnp.maximum(i - 1, 0), 0)),
            out_shape=jax.ShapeDtypeStruct((n_tok, d_model), _F32),
            scratch_shapes=[pltpu.VMEM((CONV_WIDTH // LANES, CONV_HALO + tm, LANES), _F32),
                            pltpu.VMEM((CONV_WIDTH // LANES, tm, LANES), _F32),
                            pltpu.VMEM((2, tm, d_model), _F32),
                            pltpu.VMEM((tm, d_model), _BF16)],
            compiler_params=_params(1),
            name="mixer_mlp",
        )(xf, u, u, ow, gates,
          w_dw[layer].reshape(CONV_KERNEL, n_lane_tiles, LANES).transpose(1, 0, 2),
          b_dw[layer].reshape(n_lane_tiles, 1, LANES),
          w_conv_norm[layer].reshape(1, CONV_WIDTH), w_conv_out[layer].astype(_BF16),
          w_attn_out[layer].astype(_BF16), w_o[layer].astype(_BF16),
          w_norm_mlp[layer].reshape(1, d_model),
          jnp.stack([w_mlp_in[layer][:, c * FF_CHUNK:(c + 1) * FF_CHUNK].astype(_BF16) for c in range(n_chunks)]),
          w_mlp_out[layer].astype(_BF16))

    return xf.reshape(batch, seq, d_model)
```
